```python
import math
import jax, jax.numpy as jnp
from jax import lax
import numpy as np

D_MODEL = 1024
BATCH = 4
SEQ = 4096
DEPTH = 2
DEC_BATCH = 128
DEC_SEQ = 4
PAST_LEN = 8192
PAGE_SIZE = 128

HEAD_DIM = 64
H_MLSTM = 4
H_FOX = 4
H_MLA = 8
N_HEADS = H_MLSTM + H_FOX + H_MLA
D_MIX = N_HEADS * HEAD_DIM
MLSTM_CHUNK = 128
Q_BLOCK = 128
Q_LORA = 256
KV_LORA = 128
NOPE_DIM = 64
ROPE_DIM = 32
LATENT_DIM = KV_LORA + ROPE_DIM
ROPE_THETA = 10000.0
D_FF = -(-8 * D_MODEL // (3 * 256)) * 256
ALPHA = (2 * DEPTH) ** 0.25
BETA = (8 * DEPTH) ** -0.25
LN_EPS = 1e-5
RMS_EPS = 1e-6
SPLIT = ((H_MLSTM * HEAD_DIM,) * 4 + (H_MLSTM, H_MLSTM)
         + (H_FOX * HEAD_DIM,) * 3 + (H_FOX,)
         + (Q_LORA, KV_LORA, ROPE_DIM))
D_IN = sum(SPLIT)

kernel_name = 'hybrid_mlstm_fox_mla_decoder'

F32 = jnp.float32


def _layernorm(x, g, b):
    xf = x.astype(F32)
    mu = xf.mean(-1, keepdims=True)
    var = jnp.square(xf - mu).mean(-1, keepdims=True)
    return ((xf - mu) * lax.rsqrt(var + LN_EPS) * g + b).astype(x.dtype)


def _rmsnorm(x, g):
    xf = x.astype(F32)
    return (xf * lax.rsqrt(jnp.mean(xf * xf, -1, keepdims=True) + RMS_EPS) * g).astype(x.dtype)


def _rope(x, pos):
    half = ROPE_DIM // 2
    freq = ROPE_THETA ** (-jnp.arange(half, dtype=F32) * (2.0 / ROPE_DIM))
    ang = pos.astype(F32)[:, None] * freq[None, :]
    shape = (1, pos.shape[0]) + (1,) * (x.ndim - 3) + (half,)
    cos, sin = jnp.cos(ang).reshape(shape), jnp.sin(ang).reshape(shape)
    x1, x2 = x[..., :half].astype(F32), x[..., half:].astype(F32)
    return jnp.concatenate([x1 * cos - x2 * sin, x1 * sin + x2 * cos], axis=-1).astype(x.dtype)


def _over_query_blocks(fn, *qs):
    T = qs[0].shape[1]
    if T <= Q_BLOCK or T % Q_BLOCK != 0:
        return fn(*qs)
    nb = T // Q_BLOCK
    blocks = tuple(jnp.swapaxes(a.reshape((a.shape[0], nb, Q_BLOCK) + a.shape[2:]), 0, 1) for a in qs)
    out = jnp.swapaxes(lax.map(lambda args: fn(*args), blocks), 0, 1)
    return out.reshape((out.shape[0], T) + out.shape[3:])


def _fox_attention(q, k, v, fq, fk, qpos, kpos):
    scale = HEAD_DIM ** -0.5
    fk_t = jnp.swapaxes(fk, 1, 2)

    def blk(qb, fqb, qp):
        s = jnp.einsum('bqhd,bkhd->bhqk', qb, k).astype(F32) * scale
        s = s + jnp.swapaxes(fqb, 1, 2)[..., :, None] - fk_t[..., None, :]
        s = jnp.where(kpos[None, :] <= qp[0][:, None], s, -jnp.inf)
        p = jax.nn.softmax(s, axis=-1)
        return jnp.einsum('bhqk,bkhd->bqhd', p.astype(v.dtype), v)

    return _over_query_blocks(blk, q, fq, qpos[None])


def _mla_attention(q_nope, q_rope, lat, w_uk, w_uv, qpos, kpos):
    scale = (NOPE_DIM + ROPE_DIM) ** -0.5
    ckv, kr = lat[..., :KV_LORA], lat[..., KV_LORA:]
    q_lat = jnp.einsum('bthd,chd->bthc', q_nope, w_uk)

    def blk(qlb, qrb, qp):
        s = jnp.einsum('bthc,bsc->bhts', qlb, ckv) + jnp.einsum('bthr,bsr->bhts', qrb, kr)
        s = jnp.where(kpos[None, :] <= qp[0][:, None], s.astype(F32) * scale, -jnp.inf)
        p = jax.nn.softmax(s, axis=-1)
        return jnp.einsum('bhts,bsc->bthc', p.astype(ckv.dtype), ckv)

    o_lat = _over_query_blocks(blk, q_lat, q_rope, qpos[None])
    return jnp.einsum('bthc,chd->bthd', o_lat, w_uv)


def _mlstm(q, k, v, ig, lf, c0, n0, m0):
    B, T, H, d = q.shape
    L = MLSTM_CHUNK if T % MLSTM_CHUNK == 0 else T
    nc = T // L

    def chunks(a):
        a = a.astype(F32).reshape((B, nc, L, H) + a.shape[3:])
        return jnp.moveaxis(a, (1, 3), (0, 2))

    tril = jnp.tril(jnp.ones((L, L), bool))

    def step(carry, xs):
        c, n, m = carry
        qc, kc, vc, ic, fc = xs
        b = jnp.cumsum(fc, axis=-1)
        dlog = jnp.where(tril, b[..., :, None] - b[..., None, :] + ic[..., None, :], -jnp.inf)
        inter = b + m[..., None]
        mt = jnp.maximum(inter, dlog.max(-1))
        s = jnp.einsum('bhtd,bhsd->bhts', qc, kc) * jnp.exp(dlog - mt[..., None])
        dec = jnp.exp(inter - mt)
        num = jnp.einsum('bhts,bhse->bhte', s, vc) + dec[..., None] * jnp.einsum('bhtd,bhde->bhte', qc, c)
        den = s.sum(-1) + dec * jnp.einsum('bhtd,bhd->bht', qc, n)
        h = num / jnp.maximum(jnp.abs(den), jnp.exp(-mt))[..., None]
        m_new = mt[..., -1]
        w = jnp.exp(b[..., -1:] - b + ic - m_new[..., None])
        cdec = jnp.exp(b[..., -1] + m - m_new)
        c_new = cdec[..., None, None] * c + jnp.einsum('bhs,bhsd,bhse->bhde', w, kc, vc)
        n_new = cdec[..., None] * n + jnp.einsum('bhs,bhsd->bhd', w, kc)
        return (c_new, n_new, m_new), h

    carry, h = lax.scan(step, (c0.astype(F32), n0.astype(F32), m0.astype(F32)),
                        tuple(chunks(a) for a in (q, k, v, ig, lf)))
    h = jnp.moveaxis(h, (0, 2), (1, 3)).reshape(B, T, H, d)
    return h, carry


def _layer(x, fox_k_past, fox_v_past, fox_lf_past, mla_past, c0, n0, m0,
           w_in, b_mi, b_mf, b_ff, g_cq, w_uq, g_ckv, w_uk, w_uv, g_head, w_out,
           ln1_g, ln1_b, w_gate, w_up, w_down, ln2_g, ln2_b):
    B, T, _ = x.shape
    P = fox_lf_past.shape[1]
    qpos = P + jnp.arange(T, dtype=jnp.int32)
    kpos = jnp.arange(P + T, dtype=jnp.int32)
    z = x @ w_in
    split_at = [int(i) for i in np.cumsum(SPLIT)[:-1]]
    (mq, mk, mv, mo, mi, mf, fq, fk, fv, ff, cq, ckv, kr) = jnp.split(z, split_at, axis=-1)

    def hd(a, h):
        return a.reshape(B, T, h, HEAD_DIM)

    ig = (mi + b_mi).astype(F32)
    lf = jax.nn.log_sigmoid((mf + b_mf).astype(F32))
    h_m, (c1, n1, m1) = _mlstm(hd(mq, H_MLSTM), hd(mk, H_MLSTM) * HEAD_DIM ** -0.5, hd(mv, H_MLSTM),
                               ig, lf, c0, n0, m0)
    f_lf = jax.nn.log_sigmoid((ff + b_ff).astype(F32))
    k_new, v_new = hd(fk, H_FOX), hd(fv, H_FOX)
    k_all = jnp.concatenate([fox_k_past.astype(k_new.dtype), k_new], axis=1)
    v_all = jnp.concatenate([fox_v_past.astype(v_new.dtype), v_new], axis=1)
    fcum = jnp.cumsum(jnp.concatenate([fox_lf_past.astype(F32), f_lf], axis=1), axis=1)
    h_f = _fox_attention(hd(fq, H_FOX), k_all, v_all, fcum[:, P:], fcum, qpos, kpos)
    qfull = (_rmsnorm(cq, g_cq) @ w_uq).reshape(B, T, H_MLA, NOPE_DIM + ROPE_DIM)
    q_nope, q_rope = qfull[..., :NOPE_DIM], _rope(qfull[..., NOPE_DIM:], qpos)
    lat_new = jnp.concatenate([_rmsnorm(ckv, g_ckv), _rope(kr, qpos)], axis=-1)
    lat_all = jnp.concatenate([mla_past.astype(lat_new.dtype), lat_new], axis=1)
    h_a = _mla_attention(q_nope, q_rope, lat_all, w_uk, w_uv, qpos, kpos)
    heads = jnp.concatenate([h_m.astype(x.dtype), h_f.astype(x.dtype), h_a.astype(x.dtype)], axis=2)
    heads = _rmsnorm(heads, g_head.reshape(N_HEADS, HEAD_DIM))
    heads = jnp.concatenate([heads[:, :, :H_MLSTM] * jax.nn.sigmoid(hd(mo, H_MLSTM)),
                             heads[:, :, H_MLSTM:]], axis=2)
    x = _layernorm(ALPHA * x + heads.reshape(B, T, D_MIX) @ w_out, ln1_g, ln1_b)
    ffn = (jax.nn.silu(x @ w_gate) * (x @ w_up)) @ w_down
    x = _layernorm(ALPHA * x + ffn, ln2_g, ln2_b)
    return x, k_new, v_new, f_lf, lat_new, c1, n1, m1


def setup_inputs(seed: int = 0) -> dict:
    key = jax.random.key(seed)
    keys = jax.random.split(key, 32)

    def nrm(i, shape, scale):
        return jax.random.normal(keys[i], shape, jnp.float32) * scale

    n_pages = PAST_LEN // PAGE_SIZE
    n_used = DEC_BATCH * n_pages
    n_pool = n_used + n_used // 4
    page_table = jax.random.permutation(keys[9], n_pool)[:n_used].reshape(DEC_BATCH, n_pages).astype(jnp.int32)
    fb_m = jnp.linspace(3.0, 6.0, H_MLSTM, dtype=jnp.float32)[None, :]
    fb_f = jnp.linspace(3.0, 6.0, H_FOX, dtype=jnp.float32)[None, :]
    return {
        'x_prompt': nrm(0, (BATCH, SEQ, D_MODEL), 1.0),
        'x_sample': nrm(1, (DEC_BATCH, DEC_SEQ, D_MODEL), 1.0),
        'cache_fox_k': nrm(2, (DEPTH, n_pool, PAGE_SIZE, H_FOX, HEAD_DIM), 1.0),
        'cache_fox_v': nrm(3, (DEPTH, n_pool, PAGE_SIZE, H_FOX, HEAD_DIM), 1.0),
        'cache_fox_logf': jax.nn.log_sigmoid(3.0 + 3.0 * jax.random.uniform(keys[4], (DEPTH, n_pool, PAGE_SIZE, H_FOX), jnp.float32)),
        'cache_mla_latent': nrm(5, (DEPTH, n_pool, PAGE_SIZE, LATENT_DIM), 1.0),
        'state_mlstm_C': nrm(6, (DEPTH, DEC_BATCH, H_MLSTM, HEAD_DIM, HEAD_DIM), 1.0),
        'state_mlstm_n': nrm(7, (DEPTH, DEC_BATCH, H_MLSTM, HEAD_DIM), 1.0),
        'state_mlstm_m': nrm(8, (DEPTH, DEC_BATCH, H_MLSTM), 1.0),
        'page_table': page_table,
        'w_in': nrm(10, (DEPTH, D_MODEL, D_IN), D_MODEL ** -0.5),
        'b_mlstm_i': nrm(11, (DEPTH, H_MLSTM), 0.1),
        'b_mlstm_f': fb_m + nrm(12, (DEPTH, H_MLSTM), 0.1),
        'b_fox_f': fb_f + nrm(13, (DEPTH, H_FOX), 0.1),
        'g_cq': 1.0 + nrm(14, (DEPTH, Q_LORA), 0.05),
        'w_uq': nrm(15, (DEPTH, Q_LORA, H_MLA * (NOPE_DIM + ROPE_DIM)), Q_LORA ** -0.5),
        'g_ckv': 1.0 + nrm(16, (DEPTH, KV_LORA), 0.05),
        'w_uk': nrm(17, (DEPTH, KV_LORA, H_MLA, NOPE_DIM), KV_LORA ** -0.5),
        'w_uv': nrm(18, (DEPTH, KV_LORA, H_MLA, HEAD_DIM), KV_LORA ** -0.5),
        'g_head': 1.0 + nrm(19, (DEPTH, D_MIX), 0.05),
        'w_out': nrm(20, (DEPTH, D_MIX, D_MODEL), BETA * D_MIX ** -0.5),
        'ln1_g': 1.0 + nrm(21, (DEPTH, D_MODEL), 0.05),
        'ln1_b': nrm(22, (DEPTH, D_MODEL), 0.02),
        'w_gate': nrm(23, (DEPTH, D_MODEL, D_FF), D_MODEL ** -0.5),
        'w_up': nrm(24, (DEPTH, D_MODEL, D_FF), D_MODEL ** -0.5),
        'w_down': nrm(25, (DEPTH, D_FF, D_MODEL), BETA * D_FF ** -0.5),
        'ln2_g': 1.0 + nrm(26, (DEPTH, D_MODEL), 0.05),
        'ln2_b': nrm(27, (DEPTH, D_MODEL), 0.02),
    }


def reference(x_prompt, x_sample, cache_fox_k, cache_fox_v, cache_fox_logf, cache_mla_latent,
              state_mlstm_C, state_mlstm_n, state_mlstm_m, page_table,
              w_in, b_mlstm_i, b_mlstm_f, b_fox_f, g_cq, w_uq, g_ckv, w_uk, w_uv, g_head, w_out,
              ln1_g, ln1_b, w_gate, w_up, w_down, ln2_g, ln2_b):
    n_seq, n_pages = page_table.shape

    def gather(pool, l):
        rows = pool[l, page_table]
        return rows.reshape((n_seq, n_pages * rows.shape[2]) + rows.shape[3:])

    bp = x_prompt.shape[0]
    dt = x_prompt.dtype
    yp, ys = x_prompt, x_sample
    st_p, st_s = [], []
    for l in range(DEPTH):
        w = (w_in[l], b_mlstm_i[l], b_mlstm_f[l], b_fox_f[l], g_cq[l], w_uq[l], g_ckv[l], w_uk[l], w_uv[l],
             g_head[l], w_out[l], ln1_g[l], ln1_b[l], w_gate[l], w_up[l], w_down[l], ln2_g[l], ln2_b[l])
        yp, *sp = _layer(yp,
                         jnp.zeros((bp, 0, H_FOX, HEAD_DIM), dt), jnp.zeros((bp, 0, H_FOX, HEAD_DIM), dt),
                         jnp.zeros((bp, 0, H_FOX), F32), jnp.zeros((bp, 0, LATENT_DIM), dt),
                         jnp.zeros((bp, H_MLSTM, HEAD_DIM, HEAD_DIM), F32),
                         jnp.zeros((bp, H_MLSTM, HEAD_DIM), F32), jnp.zeros((bp, H_MLSTM), F32), *w)
        ys, *ss = _layer(ys, gather(cache_fox_k, l), gather(cache_fox_v, l), gather(cache_fox_logf, l),
                         gather(cache_mla_latent, l), state_mlstm_C[l], state_mlstm_n[l], state_mlstm_m[l], *w)
        st_p.append(sp)
        st_s.append(ss)
    sp = [jnp.stack(a) for a in zip(*st_p)]
    ss = [jnp.stack(a) for a in zip(*st_s)]
    return (yp, ys, sp[0], sp[1], sp[2], sp[3], sp[4], sp[5], sp[6],
            ss[0], ss[1], ss[2], ss[3], ss[4], ss[5], ss[6])
```

```python
import functools

import jax
import jax.numpy as jnp
import numpy as np
from jax import lax
from jax.experimental import pallas as pl
from jax.experimental.pallas import tpu as pltpu

F32 = jnp.float32
BF16 = jnp.bfloat16

HEAD_DIM = 64
H_MLSTM = 4
H_FOX = 4
H_MLA = 8
MLSTM_CHUNK = 128
KV_LORA = 128
NOPE_DIM = 64
ROPE_DIM = 32
ROPE_THETA = 10000.0
LN_EPS = 1e-5
RMS_EPS = 1e-6

LANES = 128
SUBLANES = 8
VMEM_LIMIT = 56 * 1024 * 1024

C_M = 0
C_F = 1024
C_CQ = 1792
C_CKV = 2048
C_G = 2176
D_INP = 2304
G_KR = 0
G_MI = 32
G_MF = 36
G_FF = 40
G_ROWS = 16

PAD_T = 8
NEG_INF = float("-inf")


def _bf(x):
    return x.astype(BF16)


def _dot(a, b):
    return jnp.dot(a, b, preferred_element_type=F32)


def _dot_nt(a, b):
    return lax.dot_general(a, b, (((1,), (1,)), ((), ())), preferred_element_type=F32)


def _dot_tn(a, b):
    return lax.dot_general(a, b, (((0,), (0,)), ((), ())), preferred_element_type=F32)


def _split3(x):
    hi = _bf(x)
    r = x - hi.astype(F32)
    mid = _bf(r)
    lo = _bf(r - mid.astype(F32))
    return hi, mid, lo


def _cumsum_rows(tril_bf, x):
    hi, mid, lo = _split3(x)
    return _dot(tril_bf, hi) + _dot(tril_bf, mid) + _dot(tril_bf, lo)


def _cumsum_lanes(x, triu_bf):
    hi, mid, lo = _split3(x)
    return _dot(hi, triu_bf) + _dot(mid, triu_bf) + _dot(lo, triu_bf)


def _tri(n, lower):
    r = lax.broadcasted_iota(jnp.int32, (n, n), 0)
    c = lax.broadcasted_iota(jnp.int32, (n, n), 1)
    return jnp.where((r >= c) if lower else (r <= c), 1.0, 0.0).astype(BF16)


def _log_sigmoid(x):
    return jnp.minimum(x, 0.0) - jnp.log1p(jnp.exp(-jnp.abs(x)))


def _head_mask(width, h):
    lane = lax.broadcasted_iota(jnp.int32, (1, width), 1)
    return (lane // HEAD_DIM) == h


def _head_rmsnorm(x, gh, n_heads):
    width = n_heads * HEAD_DIM
    x2 = x * x
    inv = jnp.zeros_like(x)
    for h in range(n_heads):
        mk = _head_mask(width, h)
        ms = jnp.sum(jnp.where(mk, x2, 0.0), axis=1, keepdims=True) * (1.0 / HEAD_DIM)
        inv = jnp.where(mk, lax.rsqrt(ms + RMS_EPS), inv)
    return x * inv * gh


def _layernorm(x, g, b):
    mu = jnp.mean(x, axis=1, keepdims=True)
    xc = x - mu
    var = jnp.mean(xc * xc, axis=1, keepdims=True)
    return xc * lax.rsqrt(var + LN_EPS) * g + b


def _params(sem):
    return pltpu.CompilerParams(dimension_semantics=sem, vmem_limit_bytes=VMEM_LIMIT)


def _token_tile(n):
    for t in (512, 256, 128, 64, 32, 16, 8):
        if n % t == 0:
            return t
    raise ValueError(f"token count {n} is not a multiple of 8")


def _const_spec(shape):
    nd = len(shape)
    return pl.BlockSpec(shape, lambda *_: (0,) * nd)


def _inproj_kernel(x_ref, w_ref, gb_ref, zm_ref, zf_ref, zcq_ref, zckv_ref, zg_ref):
    xb = _bf(x_ref[...])
    for c in range(4):
        zm_ref[:, c * 256:(c + 1) * 256] = _dot(xb, w_ref[:, C_M + c * 256:C_M + (c + 1) * 256])
    for c in range(3):
        zf_ref[:, c * 256:(c + 1) * 256] = _dot(xb, w_ref[:, C_F + c * 256:C_F + (c + 1) * 256])
    zcq_ref[...] = _dot(xb, w_ref[:, C_CQ:C_CQ + 256])
    last = _dot(xb, w_ref[:, C_CKV:D_INP])
    zckv_ref[...] = last[:, :KV_LORA]
    g = last[:, KV_LORA:] + gb_ref[...]
    lane = lax.broadcasted_iota(jnp.int32, g.shape, 1)
    zg_ref[...] = jnp.where((lane >= G_MF) & (lane < G_FF + H_FOX), _log_sigmoid(g), g)


def _inproj(x, w_perm, gbias):
    n, d = x.shape
    tm = _token_tile(n)
    row = lambda w: pl.BlockSpec((tm, w), lambda i: (i, 0))
    widths = (1024, 768, 256, KV_LORA, LANES)
    return pl.pallas_call(
        _inproj_kernel,
        grid=(n // tm,),
        in_specs=[row(d), _const_spec(w_perm.shape), _const_spec(gbias.shape)],
        out_specs=[row(w) for w in widths],
        out_shape=[jax.ShapeDtypeStruct((n, w), F32) for w in widths],
        compiler_params=_params(("parallel",)),
        name="inproj",
    )(x, w_perm, gbias)


def _mlstm_kernel(q_ref, k_ref, v_ref, o_ref, zg_ref, gt_ref, c0_ref, n0_ref, m0_ref, gh_ref,
                  hm_ref, c1_ref, n1_ref, m1_ref, c_sc, n_sc, m_sc, *, chunk, n_valid):
    L = chunk
    W = H_MLSTM * HEAD_DIM
    ci = pl.program_id(1)

    @pl.when(ci == 0)
    def _():
        c_sc[...] = c0_ref[0]
        n_sc[...] = n0_ref[0]
        m_sc[...] = m0_ref[0]

    q = q_ref[0]
    k = k_ref[0] * (HEAD_DIM ** -0.5)
    v = v_ref[0]
    zg = zg_ref[0]
    gt = gt_ref[0]
    valid_c = lax.broadcasted_iota(jnp.int32, (L, 1), 0) < n_valid
    valid_r = lax.broadcasted_iota(jnp.int32, (1, L), 1) < n_valid
    cum_c = _cumsum_rows(_tri(L, True), jnp.where(valid_c, zg, 0.0))
    cum_r = _cumsum_lanes(jnp.where(valid_r, gt, 0.0), _tri(L, False))
    row = lax.broadcasted_iota(jnp.int32, (L, L), 0)
    col = lax.broadcasted_iota(jnp.int32, (L, L), 1)
    tril = row >= col
    qb, kb, vb = _bf(q), _bf(k), _bf(v)
    cb = _bf(c_sc[...])
    n_prev = n_sc[...]

    h_all = jnp.zeros((L, W), F32)
    w_all = jnp.zeros((L, W), F32)
    cdec_all = jnp.zeros((1, W), F32)
    for h in range(H_MLSTM):
        mk = _head_mask(W, h)
        b_c = cum_c[:, G_MF + h:G_MF + h + 1]
        b_r = cum_r[H_MLSTM + h:H_MLSTM + h + 1, :]
        ig_c = jnp.where(valid_c, zg[:, G_MI + h:G_MI + h + 1], NEG_INF)
        ig_r = jnp.where(valid_r, gt[h:h + 1, :], NEG_INF)
        m_prev = m_sc[h:h + 1, 0:1]
        dlog = jnp.where(tril, b_c - b_r + ig_r, NEG_INF)
        inter = b_c + m_prev
        mt = jnp.maximum(inter, jnp.max(dlog, axis=1, keepdims=True))
        dmat = jnp.exp(dlog - mt)
        qh = jnp.where(mk, q, 0.0)
        qhb = _bf(qh)
        s = _dot_nt(qhb, kb) * dmat
        dec = jnp.exp(inter - mt)
        num = _dot(_bf(s), vb) + dec * _dot(qhb, cb)
        den = jnp.sum(s, axis=1, keepdims=True) + dec * jnp.sum(qh * n_prev, axis=1, keepdims=True)
        hh = num / jnp.maximum(jnp.abs(den), jnp.exp(-mt))
        h_all = jnp.where(mk, hh, h_all)
        m_new = mt[L - 1:L, :]
        b_last = b_c[L - 1:L, :]
        w_c = jnp.exp(b_last - b_c + ig_c - m_new)
        cdec = jnp.exp(b_last + m_prev - m_new)
        w_all = jnp.where(mk, w_c, w_all)
        cdec_all = jnp.where(mk, cdec, cdec_all)
        m_sc[h:h + 1, :] = jnp.broadcast_to(m_new, (1, LANES))

    kw = k * w_all
    r2 = lax.broadcasted_iota(jnp.int32, (W, W), 0) // HEAD_DIM
    c2 = lax.broadcasted_iota(jnp.int32, (W, W), 1) // HEAD_DIM
    c_sc[...] = jnp.where(r2 == c2, cdec_all * c_sc[...] + _dot_tn(_bf(kw), vb), 0.0)
    n_sc[...] = cdec_all * n_prev + jnp.sum(kw, axis=0, keepdims=True)

    hm_ref[0] = _head_rmsnorm(h_all, gh_ref[...], H_MLSTM) * jax.nn.sigmoid(o_ref[0])

    @pl.when(ci == pl.num_programs(1) - 1)
    def _():
        c1_ref[0] = c_sc[...]
        n1_ref[0] = n_sc[...]
        m1_ref[0] = m_sc[...]


def _mlstm(zm3, zg3, gt3, c0, n0, m0, gh, n_valid):
    b, t, _ = zm3.shape
    chunk = MLSTM_CHUNK if t % MLSTM_CHUNK == 0 else t
    nc = t // chunk
    w = H_MLSTM * HEAD_DIM
    col = lambda j: pl.BlockSpec((1, chunk, w), lambda bi, ci: (bi, ci, j))
    state = lambda shape: pl.BlockSpec((1,) + shape, lambda bi, ci: (bi, 0, 0))
    return pl.pallas_call(
        functools.partial(_mlstm_kernel, chunk=chunk, n_valid=n_valid),
        grid=(b, nc),
        in_specs=[col(0), col(1), col(2), col(3),
                  pl.BlockSpec((1, chunk, LANES), lambda bi, ci: (bi, ci, 0)),
                  pl.BlockSpec((1, G_ROWS, chunk), lambda bi, ci: (bi, 0, ci)),
                  state((w, w)), state((1, w)), state((SUBLANES, LANES)),
                  _const_spec((1, w))],
        out_specs=[pl.BlockSpec((1, chunk, w), lambda bi, ci: (bi, ci, 0)),
                   state((w, w)), state((1, w)), state((SUBLANES, LANES))],
        out_shape=[jax.ShapeDtypeStruct((b, t, w), F32),
                   jax.ShapeDtypeStruct((b, w, w), F32),
                   jax.ShapeDtypeStruct((b, 1, w), F32),
                   jax.ShapeDtypeStruct((b, SUBLANES, LANES), F32)],
        scratch_shapes=[pltpu.VMEM((w, w), F32), pltpu.VMEM((1, w), F32), pltpu.VMEM((SUBLANES, LANES), F32)],
        compiler_params=_params(("parallel", "arbitrary")),
        name="mlstm",
    )(zm3, zm3, zm3, zm3, zg3, gt3, c0, n0, m0, gh)


def _fcum_kernel(zg_ref, gt_ref, fc_ref, fr_ref, cc_sc, cr_sc, *, chunk):
    L = chunk

    @pl.when(pl.program_id(1) == 0)
    def _():
        cc_sc[...] = jnp.zeros_like(cc_sc)
        cr_sc[...] = jnp.zeros_like(cr_sc)

    cum_c = _cumsum_rows(_tri(L, True), zg_ref[0]) + cc_sc[...]
    cum_r = _cumsum_lanes(gt_ref[0], _tri(L, False)) + cr_sc[:, 0:1]
    fc_ref[0] = cum_c
    fr_ref[0] = cum_r
    cc_sc[...] = cum_c[L - 1:L, :]
    cr_sc[...] = jnp.broadcast_to(cum_r[:, L - 1:L], cr_sc.shape)


def _fcum(zg3, gt3):
    b, t, _ = zg3.shape
    chunk = LANES if t % LANES == 0 else t
    return pl.pallas_call(
        functools.partial(_fcum_kernel, chunk=chunk),
        grid=(b, t // chunk),
        in_specs=[pl.BlockSpec((1, chunk, LANES), lambda bi, ci: (bi, ci, 0)),
                  pl.BlockSpec((1, G_ROWS, chunk), lambda bi, ci: (bi, 0, ci))],
        out_specs=[pl.BlockSpec((1, chunk, LANES), lambda bi, ci: (bi, ci, 0)),
                   pl.BlockSpec((1, G_ROWS, chunk), lambda bi, ci: (bi, 0, ci))],
        out_shape=[jax.ShapeDtypeStruct(zg3.shape, F32), jax.ShapeDtypeStruct(gt3.shape, F32)],
        scratch_shapes=[pltpu.VMEM((1, LANES), F32), pltpu.VMEM((G_ROWS, LANES), F32)],
        compiler_params=_params(("parallel", "arbitrary")),
        name="fcum",
    )(zg3, gt3)


def _fox_kernel(q_ref, k_ref, v_ref, fc_ref, fr_ref, gh_ref, o_ref, acc_sc, m_sc, l_sc, *, tq, tk):
    W = H_FOX * HEAD_DIM
    qi = pl.program_id(1)
    ki = pl.program_id(2)
    last_k = (qi * tq + tq - 1) // tk

    @pl.when(ki == 0)
    def _():
        acc_sc[...] = jnp.zeros_like(acc_sc)
        m_sc[...] = jnp.full_like(m_sc, NEG_INF)
        l_sc[...] = jnp.zeros_like(l_sc)

    @pl.when(ki <= last_k)
    def _():
        q = q_ref[0]
        kb = _bf(k_ref[0])
        vb = _bf(v_ref[0])
        fc = fc_ref[0]
        fr = fr_ref[0]
        qpos = qi * tq + lax.broadcasted_iota(jnp.int32, (tq, tk), 0)
        kpos = ki * tk + lax.broadcasted_iota(jnp.int32, (tq, tk), 1)
        causal = kpos <= qpos
        acc = acc_sc[...]
        for h in range(H_FOX):
            mk = _head_mask(W, h)
            qh = _bf(jnp.where(mk, q, 0.0))
            s = _dot_nt(qh, kb) * (HEAD_DIM ** -0.5)
            s = s + fc[:, G_FF + h:G_FF + h + 1] - fr[2 * H_MLSTM + h:2 * H_MLSTM + h + 1, :]
            s = jnp.where(causal, s, NEG_INF)
            m_prev = m_sc[h]
            m_new = jnp.maximum(m_prev, jnp.max(s, axis=1, keepdims=True))
            alpha = jnp.exp(m_prev - m_new)
            p = jnp.exp(s - m_new)
            l_sc[h] = alpha * l_sc[h] + jnp.sum(p, axis=1, keepdims=True)
            m_sc[h] = m_new
            acc = jnp.where(mk, alpha * acc + _dot(_bf(p), vb), acc)
        acc_sc[...] = acc

    @pl.when(ki == last_k)
    def _():
        acc = acc_sc[...]
        inv_l = jnp.zeros_like(acc)
        for h in range(H_FOX):
            inv_l = jnp.where(_head_mask(W, h), 1.0 / l_sc[h], inv_l)
        o_ref[0] = _head_rmsnorm(acc * inv_l, gh_ref[...], H_FOX)


def _fox_prompt(zf3, fc3, fr3, gh):
    b, t, _ = zf3.shape
    w = H_FOX * HEAD_DIM
    tq = tk = 256 if t % 256 == 0 else t
    last = lambda qi: (qi * tq + tq - 1) // tk
    return pl.pallas_call(
        functools.partial(_fox_kernel, tq=tq, tk=tk),
        grid=(b, t // tq, t // tk),
        in_specs=[pl.BlockSpec((1, tq, w), lambda bi, qi, ki: (bi, qi, 0)),
                  pl.BlockSpec((1, tk, w), lambda bi, qi, ki: (bi, jnp.minimum(ki, last(qi)), 1)),
                  pl.BlockSpec((1, tk, w), lambda bi, qi, ki: (bi, jnp.minimum(ki, last(qi)), 2)),
                  pl.BlockSpec((1, tq, LANES), lambda bi, qi, ki: (bi, qi, 0)),
                  pl.BlockSpec((1, G_ROWS, tk), lambda bi, qi, ki: (bi, 0, jnp.minimum(ki, last(qi)))),
                  _const_spec((1, w))],
        out_specs=pl.BlockSpec((1, tq, w), lambda bi, qi, ki: (bi, qi, 0)),
        out_shape=jax.ShapeDtypeStruct((b, t, w), F32),
        scratch_shapes=[pltpu.VMEM((tq, w), F32), pltpu.VMEM((H_FOX, tq, 1), F32), pltpu.VMEM((H_FOX, tq, 1), F32)],
        compiler_params=_params(("parallel", "parallel", "arbitrary")),
        name="fox_prompt",
    )(zf3, zf3, zf3, fc3, fr3, gh)


def _rope_swap(x, group):
    w = x.shape[1]
    half = ROPE_DIM // 2
    lane = lax.broadcasted_iota(jnp.int32, x.shape, 1)
    first = (lane % group) < half
    return jnp.where(first, pltpu.roll(x, w - half, 1), pltpu.roll(x, half, 1))


def _mla_pre_kernel(zcq_ref, zckv_ref, zg_ref, cs_ref, sn_ref, gcq_ref, wn_ref, wr_ref, gckv_ref, wuk_ref, sel_ref,
                    q_ref, lat_ref, latp_ref):
    zcq = zcq_ref[...]
    cqn = zcq * lax.rsqrt(jnp.mean(zcq * zcq, axis=1, keepdims=True) + RMS_EPS) * gcq_ref[...]
    cqb = _bf(cqn)
    qn = _dot(cqb, wn_ref[...])
    qr = _dot(cqb, wr_ref[...])
    cs = cs_ref[...]
    sn = sn_ref[...]
    qrot = qr * cs + _rope_swap(qr, ROPE_DIM) * sn
    qlat = _dot(_bf(qn), wuk_ref[...])
    qrope = _dot(_bf(qrot), sel_ref[...])
    for h in range(H_MLA):
        q_ref[0, h, :, 0:LANES] = qlat[:, h * LANES:(h + 1) * LANES].astype(q_ref.dtype)
        q_ref[0, h, :, LANES:2 * LANES] = qrope[:, h * LANES:(h + 1) * LANES].astype(q_ref.dtype)

    zckv = zckv_ref[...]
    ckvn = zckv * lax.rsqrt(jnp.mean(zckv * zckv, axis=1, keepdims=True) + RMS_EPS) * gckv_ref[...]
    zg = zg_ref[...]
    krot = zg * cs[:, :LANES] + _rope_swap(zg, ROPE_DIM) * sn[:, :LANES]
    lane = lax.broadcasted_iota(jnp.int32, krot.shape, 1)
    krz = jnp.where(lane < ROPE_DIM, krot, 0.0)
    lat_ref[:, 0:KV_LORA] = ckvn
    lat_ref[:, KV_LORA:KV_LORA + ROPE_DIM] = krot[:, 0:ROPE_DIM]
    latp_ref[:, 0:KV_LORA] = _bf(ckvn)
    latp_ref[:, KV_LORA:2 * KV_LORA] = _bf(krz)


def _mla_pre(zcq, zckv, zg, cs, sn, lw, bq, tq, q_dtype):
    n = zcq.shape[0]
    tm = _token_tile(min(n, tq))
    tm = min(tm, tq)
    nt_tab = cs.shape[0] // tm
    nt_q = tq // tm
    row = lambda w: pl.BlockSpec((tm, w), lambda i: (i, 0))
    tab = pl.BlockSpec((tm, 2 * LANES), lambda i: (i % nt_tab, 0))
    consts = (lw["g_cq"], lw["w_uq_n"], lw["w_uq_r"], lw["g_ckv"], lw["wuk_bd"], lw["sel"])
    return pl.pallas_call(
        _mla_pre_kernel,
        grid=(n // tm,),
        in_specs=[row(256), row(KV_LORA), row(LANES), tab, tab] + [_const_spec(c.shape) for c in consts],
        out_specs=[pl.BlockSpec((1, H_MLA, tm, 2 * LANES), lambda i: (i // nt_q, 0, i % nt_q, 0)),
                   row(KV_LORA + ROPE_DIM), row(2 * LANES)],
        out_shape=[jax.ShapeDtypeStruct((bq, H_MLA, tq, 2 * LANES), q_dtype),
                   jax.ShapeDtypeStruct((n, KV_LORA + ROPE_DIM), F32),
                   jax.ShapeDtypeStruct((n, 2 * LANES), BF16)],
        compiler_params=_params(("parallel",)),
        name="mla_pre",
    )(zcq, zckv, zg, cs, sn, *consts)


def _mla_out(acc, inv_l, wuv_ref, gh, rows):
    out = jnp.zeros((rows, H_MLA * HEAD_DIM), F32)
    for h in range(H_MLA):
        o_h = acc[h * rows:(h + 1) * rows] * inv_l[h * rows:(h + 1) * rows]
        out = out + _dot(_bf(o_h), wuv_ref[h])
    return _head_rmsnorm(out, gh, H_MLA)


def _mla_kernel(q_ref, k_ref, wuv_ref, gh_ref, o_ref, acc_sc, m_sc, l_sc, *, tq, tk):
    qi = pl.program_id(1)
    ki = pl.program_id(2)
    last_k = (qi * tq + tq - 1) // tk
    rows = H_MLA * tq

    @pl.when(ki == 0)
    def _():
        acc_sc[...] = jnp.zeros_like(acc_sc)
        m_sc[...] = jnp.full_like(m_sc, NEG_INF)
        l_sc[...] = jnp.zeros_like(l_sc)

    @pl.when(ki <= last_k)
    def _():
        q = q_ref[0].reshape(rows, 2 * LANES)
        kb = k_ref[0]
        s = _dot_nt(q, kb) * ((NOPE_DIM + ROPE_DIM) ** -0.5)
        qpos = qi * tq + lax.broadcasted_iota(jnp.int32, (rows, tk), 0) % tq
        kpos = ki * tk + lax.broadcasted_iota(jnp.int32, (rows, tk), 1)
        s = jnp.where(kpos <= qpos, s, NEG_INF)
        m_prev = m_sc[...]
        m_new = jnp.maximum(m_prev, jnp.max(s, axis=1, keepdims=True))
        alpha = jnp.exp(m_prev - m_new)
        p = jnp.exp(s - m_new)
        l_sc[...] = alpha * l_sc[...] + jnp.sum(p, axis=1, keepdims=True)
        m_sc[...] = m_new
        acc_sc[...] = alpha * acc_sc[...] + _dot(_bf(p), kb[:, :KV_LORA])

    @pl.when(ki == last_k)
    def _():
        o_ref[0] = _mla_out(acc_sc[...], 1.0 / l_sc[...], wuv_ref, gh_ref[...], tq)


def _mla_prompt(qp, latp3, wuv_pad, gh):
    b, _, t, _ = qp.shape
    tq = 128 if t % 128 == 0 else t
    tk = 256 if t % 256 == 0 else t
    last = lambda qi: (qi * tq + tq - 1) // tk
    w = H_MLA * HEAD_DIM
    return pl.pallas_call(
        functools.partial(_mla_kernel, tq=tq, tk=tk),
        grid=(b, t // tq, t // tk),
        in_specs=[pl.BlockSpec((1, H_MLA, tq, 2 * LANES), lambda bi, qi, ki: (bi, 0, qi, 0)),
                  pl.BlockSpec((1, tk, 2 * LANES), lambda bi, qi, ki: (bi, jnp.minimum(ki, last(qi)), 0)),
                  _const_spec(wuv_pad.shape), _const_spec((1, w))],
        out_specs=pl.BlockSpec((1, tq, w), lambda bi, qi, ki: (bi, qi, 0)),
        out_shape=jax.ShapeDtypeStruct((b, t, w), F32),
        scratch_shapes=[pltpu.VMEM((H_MLA * tq, KV_LORA), F32), pltpu.VMEM((H_MLA * tq, 1), F32),
                        pltpu.VMEM((H_MLA * tq, 1), F32)],
        compiler_params=_params(("parallel", "parallel", "arbitrary")),
        name="mla_prompt",
    )(qp, latp3, wuv_pad, gh)


def _decode_kernel(pt_ref, *refs, n_group):
    G = n_group
    kt_refs = refs[0:G]
    vt_refs = refs[G:2 * G]
    lt_refs = refs[2 * G:3 * G]
    lf_refs = refs[3 * G:4 * G]
    (ktn_ref, vtn_ref, ltn_ref, lfn_ref, qf_ref, qa_ref, wuv_ref, ghf_ref, gha_ref,
     of_ref, oa_ref,
     facc, fm, fl, aacc, am, al, carry, kbuf, lfbuf) = refs[4 * G:]
    del pt_ref
    j = pl.program_id(1)
    nj = pl.num_programs(1)
    WF = H_FOX * HEAD_DIM
    RF = H_FOX * PAD_T
    RA = H_MLA * PAD_T

    @pl.when(j == 0)
    def _():
        facc[...] = jnp.zeros_like(facc)
        fm[...] = jnp.full_like(fm, NEG_INF)
        fl[...] = jnp.zeros_like(fl)
        aacc[...] = jnp.zeros_like(aacc)
        am[...] = jnp.full_like(am, NEG_INF)
        al[...] = jnp.zeros_like(al)
        carry[...] = jnp.zeros_like(carry)
        kbuf[...] = jnp.zeros_like(kbuf)
        lfbuf[...] = jnp.zeros_like(lfbuf)

    q = qf_ref[0]
    qbd = jnp.concatenate([_bf(jnp.where(_head_mask(WF, h), q, 0.0)) for h in range(H_FOX)], axis=0)
    qa = _bf(qa_ref[0].reshape(RA, 2 * LANES))
    triu = _tri(LANES, False)

    def block(kts, vts, lts, lfs, new_tokens):
        n = len(kts)
        for g in range(n):
            lfbuf[g * SUBLANES:g * SUBLANES + H_FOX, :] = lfs[g][...]
        cs = _cumsum_lanes(lfbuf[0:n * SUBLANES, :], triu)
        s_f, s_a = [], []
        c = carry[:, 0:1]
        for g in range(n):
            cum = cs[g * SUBLANES:(g + 1) * SUBLANES] + c
            c = cum[:, LANES - 1:LANES]
            bias = jnp.concatenate([jnp.broadcast_to(cum[h:h + 1, :], (PAD_T, LANES)) for h in range(H_FOX)], axis=0)
            s_f.append(_dot(qbd, _bf(kts[g][...])) * (HEAD_DIM ** -0.5) - bias)
            kbuf[0:KV_LORA + ROPE_DIM, :] = _bf(lts[g][...])
            s_a.append(_dot(qa, kbuf[...]) * ((NOPE_DIM + ROPE_DIM) ** -0.5))
        if not new_tokens:
            carry[...] = jnp.broadcast_to(c, carry.shape)
        s_f = jnp.concatenate(s_f, axis=1)
        s_a = jnp.concatenate(s_a, axis=1)
        if new_tokens:
            ok_f = lax.broadcasted_iota(jnp.int32, s_f.shape, 1) <= lax.broadcasted_iota(jnp.int32, s_f.shape, 0) % PAD_T
            ok_a = lax.broadcasted_iota(jnp.int32, s_a.shape, 1) <= lax.broadcasted_iota(jnp.int32, s_a.shape, 0) % PAD_T
            s_f = jnp.where(ok_f, s_f, NEG_INF)
            s_a = jnp.where(ok_a, s_a, NEG_INF)
        m_prev = fm[...]
        m_new = jnp.maximum(m_prev, jnp.max(s_f, axis=1, keepdims=True))
        alpha = jnp.exp(m_prev - m_new)
        p = jnp.exp(s_f - m_new)
        fl[...] = alpha * fl[...] + jnp.sum(p, axis=1, keepdims=True)
        fm[...] = m_new
        pv = jnp.zeros((RF, WF), F32)
        for g in range(n):
            pv = pv + _dot_nt(_bf(p[:, g * LANES:(g + 1) * LANES]), _bf(vts[g][...]))
        facc[...] = alpha * facc[...] + pv
        m_prev = am[...]
        m_new = jnp.maximum(m_prev, jnp.max(s_a, axis=1, keepdims=True))
        alpha = jnp.exp(m_prev - m_new)
        p = jnp.exp(s_a - m_new)
        al[...] = alpha * al[...] + jnp.sum(p, axis=1, keepdims=True)
        am[...] = m_new
        pv = jnp.zeros((RA, KV_LORA), F32)
        for g in range(n):
            pv = pv + _dot_nt(_bf(p[:, g * LANES:(g + 1) * LANES]), _bf(lts[g][0:KV_LORA, :]))
        aacc[...] = alpha * aacc[...] + pv

    block([r.at[0, 0] for r in kt_refs], [r.at[0, 0] for r in vt_refs],
          [r.at[0, 0] for r in lt_refs], [r.at[0, 0] for r in lf_refs], False)

    @pl.when(j == nj - 1)
    def _():
        block([ktn_ref.at[0]], [vtn_ref.at[0]], [ltn_ref.at[0]], [lfn_ref.at[0]], True)
        accf = facc[...] / fl[...]
        of = jnp.zeros((PAD_T, WF), F32)
        for h in range(H_FOX):
            of = jnp.where(_head_mask(WF, h), accf[h * PAD_T:(h + 1) * PAD_T], of)
        of_ref[0] = _head_rmsnorm(of, ghf_ref[...], H_FOX)
        oa_ref[0] = _mla_out(aacc[...], 1.0 / al[...], wuv_ref, gha_ref[...], PAD_T)


def _decode(page_table, layer, caches, new_pages, qf, qa, wuv_pad, ghf, gha):
    b, n_pages = page_table.shape
    G = 8
    while n_pages % G:
        G //= 2
    kt, vt, lt, lf = caches
    page = kt.shape[-1]

    def pool_spec(arr, g):
        rows = arr.shape[2]
        return pl.BlockSpec((1, 1, rows, page), lambda bi, ji, pt: (layer, pt[bi, ji * G + g], 0, 0))

    def seq_spec(arr):
        nd = arr.ndim
        return pl.BlockSpec((1,) + arr.shape[1:], lambda bi, ji, pt: (bi,) + (0,) * (nd - 1))

    def const(arr):
        nd = arr.ndim
        return pl.BlockSpec(arr.shape, lambda bi, ji, pt: (0,) * nd)

    in_specs = []
    operands = []
    for arr in (kt, vt, lt, lf):
        for g in range(G):
            in_specs.append(pool_spec(arr, g))
            operands.append(arr)
    for arr in tuple(new_pages) + (qf, qa):
        in_specs.append(seq_spec(arr))
        operands.append(arr)
    for arr in (wuv_pad, ghf, gha):
        in_specs.append(const(arr))
        operands.append(arr)
    wf = H_FOX * HEAD_DIM
    wa = H_MLA * HEAD_DIM
    grid_spec = pltpu.PrefetchScalarGridSpec(
        num_scalar_prefetch=1,
        grid=(b, n_pages // G),
        in_specs=in_specs,
        out_specs=[pl.BlockSpec((1, PAD_T, wf), lambda bi, ji, pt: (bi, 0, 0)),
                   pl.BlockSpec((1, PAD_T, wa), lambda bi, ji, pt: (bi, 0, 0))],
        scratch_shapes=[pltpu.VMEM((H_FOX * PAD_T, wf), F32), pltpu.VMEM((H_FOX * PAD_T, 1), F32),
                        pltpu.VMEM((H_FOX * PAD_T, 1), F32),
                        pltpu.VMEM((H_MLA * PAD_T, KV_LORA), F32), pltpu.VMEM((H_MLA * PAD_T, 1), F32),
                        pltpu.VMEM((H_MLA * PAD_T, 1), F32),
                        pltpu.VMEM((SUBLANES, LANES), F32), pltpu.VMEM((2 * LANES, page), BF16),
                        pltpu.VMEM((G * SUBLANES, LANES), F32)])
    return pl.pallas_call(
        functools.partial(_decode_kernel, n_group=G),
        grid_spec=grid_spec,
        out_shape=[jax.ShapeDtypeStruct((b, PAD_T, wf), F32), jax.ShapeDtypeStruct((b, PAD_T, wa), F32)],
        compiler_params=_params(("parallel", "arbitrary")),
        name="decode",
    )(page_table, *operands)


def _out_ffn_kernel(x_ref, hm_ref, hf_ref, ha_ref, wom_ref, wof_ref, woa_ref, g1_ref, b1_ref,
                    wg_ref, wu_ref, wd_ref, g2_ref, b2_ref, y_ref, acc_sc, *, alpha, n_chunks):
    mix = (_dot(_bf(hm_ref[...]), wom_ref[...]) + _dot(_bf(hf_ref[...]), wof_ref[...])
           + _dot(_bf(ha_ref[...]), woa_ref[...]))
    x1 = _layernorm(alpha * x_ref[...] + mix, g1_ref[...], b1_ref[...])
    x1b = _bf(x1)
    acc_sc[...] = jnp.zeros_like(acc_sc)

    def body(c, carry):
        g = _dot(x1b, wg_ref[c])
        u = _dot(x1b, wu_ref[c])
        acc_sc[...] += _dot(_bf(g * jax.nn.sigmoid(g) * u), wd_ref[c])
        return carry

    lax.fori_loop(0, n_chunks, body, 0)
    y_ref[...] = _layernorm(alpha * x1 + acc_sc[...], g2_ref[...], b2_ref[...])


def _out_ffn(x, hm, hf, ha, lw, alpha):
    n, d = x.shape
    tm = _token_tile(n)
    row = lambda w: pl.BlockSpec((tm, w), lambda i: (i, 0))
    consts = (lw["wo_m"], lw["wo_f"], lw["wo_a"], lw["ln1_g"], lw["ln1_b"],
              lw["wg"], lw["wu"], lw["wd"], lw["ln2_g"], lw["ln2_b"])

    def resident(c):
        nd = c.ndim
        return pl.BlockSpec(c.shape, lambda i: (0,) * nd, pipeline_mode=pl.Buffered(1))

    return pl.pallas_call(
        functools.partial(_out_ffn_kernel, alpha=alpha, n_chunks=lw["wg"].shape[0]),
        grid=(n // tm,),
        in_specs=[row(d), row(hm.shape[1]), row(hf.shape[1]), row(ha.shape[1])] + [resident(c) for c in consts],
        out_specs=row(d),
        out_shape=jax.ShapeDtypeStruct((n, d), F32),
        scratch_shapes=[pltpu.VMEM((tm, d), F32)],
        compiler_params=_params(("parallel",)),
        name="out_ffn",
    )(x, hm, hf, ha, *consts)


FF_CHUNK = 256


def _prep_layer(l, w_in, b_mi, b_mf, b_ff, g_cq, w_uq, g_ckv, w_uk, w_uv, g_head, w_out,
                ln1_g, ln1_b, w_gate, w_up, w_down, ln2_g, ln2_b):
    d = w_in.shape[1]
    wm = H_MLSTM * HEAD_DIM
    wf = H_FOX * HEAD_DIM
    q_lora = g_cq.shape[1]
    split = (wm,) * 4 + (H_MLSTM, H_MLSTM) + (wf,) * 3 + (H_FOX,) + (q_lora, KV_LORA, ROPE_DIM)
    at = [int(i) for i in np.cumsum(split)[:-1]]
    mq, mk, mv, mo, mi, mf, fq, fk, fv, ff, cq, ckv, kr = jnp.split(w_in[l], at, axis=1)
    pad = jnp.zeros((d, LANES - ROPE_DIM - 2 * H_MLSTM - H_FOX), F32)
    w_perm = _bf(jnp.concatenate([mq, mk, mv, mo, fq, fk, fv, cq, ckv, kr, mi, mf, ff, pad], axis=1))
    gbias = jnp.zeros((1, LANES), F32)
    gbias = gbias.at[0, G_MI:G_MI + H_MLSTM].set(b_mi[l]).at[0, G_MF:G_MF + H_MLSTM].set(b_mf[l])
    gbias = gbias.at[0, G_FF:G_FF + H_FOX].set(b_ff[l])
    uq = w_uq[l].reshape(q_lora, H_MLA, NOPE_DIM + ROPE_DIM)
    eye = jnp.eye(H_MLA, dtype=F32)
    sel = np.zeros((H_MLA * ROPE_DIM, H_MLA * LANES), np.float32)
    for h in range(H_MLA):
        sel[h * ROPE_DIM + np.arange(ROPE_DIM), h * LANES + np.arange(ROPE_DIM)] = 1.0
    n_ff = w_gate.shape[2] // FF_CHUNK
    return dict(
        w_perm=w_perm, gbias=gbias,
        g_cq=g_cq[l][None], g_ckv=g_ckv[l][None],
        w_uq_n=_bf(uq[:, :, :NOPE_DIM].reshape(q_lora, H_MLA * NOPE_DIM)),
        w_uq_r=_bf(uq[:, :, NOPE_DIM:].reshape(q_lora, H_MLA * ROPE_DIM)),
        wuk_bd=_bf(jnp.einsum("chd,hg->hdgc", w_uk[l], eye).reshape(H_MLA * NOPE_DIM, H_MLA * KV_LORA)),
        sel=jnp.asarray(sel, BF16),
        wuv_pad=_bf(jnp.einsum("chd,hg->hcgd", w_uv[l], eye).reshape(H_MLA, KV_LORA, H_MLA * HEAD_DIM)),
        gh_m=g_head[l][None, :wm], gh_f=g_head[l][None, wm:wm + wf], gh_a=g_head[l][None, wm + wf:],
        wo_m=_bf(w_out[l][:wm]), wo_f=_bf(w_out[l][wm:wm + wf]), wo_a=_bf(w_out[l][wm + wf:]),
        ln1_g=ln1_g[l][None], ln1_b=ln1_b[l][None], ln2_g=ln2_g[l][None], ln2_b=ln2_b[l][None],
        wg=_bf(w_gate[l].reshape(d, n_ff, FF_CHUNK).transpose(1, 0, 2)),
        wu=_bf(w_up[l].reshape(d, n_ff, FF_CHUNK).transpose(1, 0, 2)),
        wd=_bf(w_down[l].reshape(n_ff, FF_CHUNK, d)),
    )


def _rope_tables(pos):
    half = ROPE_DIM // 2
    freq = ROPE_THETA ** (-jnp.arange(half, dtype=F32) * (2.0 / ROPE_DIM))
    ang = pos.astype(F32)[:, None] * freq[None, :]
    cos, sin = jnp.cos(ang), jnp.sin(ang)
    reps = 2 * LANES // ROPE_DIM
    return (jnp.tile(jnp.concatenate([cos, cos], axis=1), (1, reps)),
            jnp.tile(jnp.concatenate([-sin, sin], axis=1), (1, reps)))


def _block_diag_state(c):
    b = c.shape[0]
    eye = jnp.eye(H_MLSTM, dtype=c.dtype)
    return jnp.einsum("bhde,hg->bhdge", c, eye).reshape(b, H_MLSTM * HEAD_DIM, H_MLSTM * HEAD_DIM)


def _diag_blocks(cbd):
    b = cbd.shape[0]
    c5 = cbd.reshape(b, H_MLSTM, HEAD_DIM, H_MLSTM, HEAD_DIM)
    return jnp.stack([c5[:, h, :, h, :] for h in range(H_MLSTM)], axis=1)


def _layer_common(x3, lw, cs, sn, c0, n0, m0, n_valid, mla_bq, mla_tq, q_dtype):
    b, t, d = x3.shape
    zm, zf, zcq, zckv, zg = _inproj(x3.reshape(b * t, d), lw["w_perm"], lw["gbias"])
    zg3 = zg.reshape(b, t, LANES)
    gt3 = jnp.swapaxes(zg3[:, :, G_MI:G_MI + G_ROWS], 1, 2)
    hm, c1, n1, m1 = _mlstm(zm.reshape(b, t, -1), zg3, gt3, c0, n0, m0, lw["gh_m"], n_valid)
    qp, lat, latp = _mla_pre(zcq, zckv, zg, cs, sn, lw, mla_bq, mla_tq, q_dtype)
    return zf, zg3, gt3, hm, (c1, n1, m1), qp, lat, latp


def _state_out(c1, n1, m1):
    b = c1.shape[0]
    return (_diag_blocks(c1), n1.reshape(b, H_MLSTM, HEAD_DIM), m1[:, :H_MLSTM, 0])


def kernel(x_prompt, x_sample, cache_fox_k, cache_fox_v, cache_fox_logf, cache_mla_latent, state_mlstm_C, state_mlstm_n, state_mlstm_m, page_table, w_in, b_mlstm_i, b_mlstm_f, b_fox_f, g_cq, w_uq, g_ckv, w_uk, w_uv, g_head, w_out, ln1_g, ln1_b, w_gate, w_up, w_down, ln2_g, ln2_b):
    depth = w_in.shape[0]
    bp, tp, d = x_prompt.shape
    bs, ts, _ = x_sample.shape
    n_pages = page_table.shape[1]
    page = cache_fox_k.shape[2]
    past = n_pages * page
    alpha = (2 * depth) ** 0.25
    wm = H_MLSTM * HEAD_DIM
    wf = H_FOX * HEAD_DIM

    pools = (jnp.transpose(cache_fox_k, (0, 1, 3, 4, 2)).reshape(depth, -1, wf, page),
             jnp.transpose(cache_fox_v, (0, 1, 3, 4, 2)).reshape(depth, -1, wf, page),
             jnp.transpose(cache_mla_latent, (0, 1, 3, 2)),
             jnp.transpose(cache_fox_logf, (0, 1, 3, 2)))

    cs_p, sn_p = _rope_tables(jnp.arange(tp, dtype=jnp.int32))
    cs_s, sn_s = _rope_tables(past + jnp.arange(PAD_T, dtype=jnp.int32))
    cs_s, sn_s = jnp.tile(cs_s, (bs, 1)), jnp.tile(sn_s, (bs, 1))

    yp = x_prompt
    ys = jnp.pad(x_sample, ((0, 0), (0, PAD_T - ts), (0, 0)))
    outs_p, outs_s = [], []
    for l in range(depth):
        lw = _prep_layer(l, w_in, b_mlstm_i, b_mlstm_f, b_fox_f, g_cq, w_uq, g_ckv, w_uk, w_uv, g_head, w_out,
                         ln1_g, ln1_b, w_gate, w_up, w_down, ln2_g, ln2_b)

        c0 = jnp.zeros((bp, wm, wm), F32)
        n0 = jnp.zeros((bp, 1, wm), F32)
        m0 = jnp.zeros((bp, SUBLANES, LANES), F32)
        zf, zg3, gt3, hm, st, qp, lat, latp = _layer_common(yp, lw, cs_p, sn_p, c0, n0, m0, tp, bp, tp, BF16)
        fc3, fr3 = _fcum(zg3, gt3)
        zf3 = zf.reshape(bp, tp, -1)
        hf = _fox_prompt(zf3, fc3, fr3, lw["gh_f"])
        ha = _mla_prompt(qp, latp.reshape(bp, tp, -1), lw["wuv_pad"], lw["gh_a"])
        yp = _out_ffn(yp.reshape(bp * tp, d), hm.reshape(bp * tp, -1), hf.reshape(bp * tp, -1),
                      ha.reshape(bp * tp, -1), lw, alpha).reshape(bp, tp, d)
        outs_p.append((zf3[:, :, wf:2 * wf].reshape(bp, tp, H_FOX, HEAD_DIM),
                       zf3[:, :, 2 * wf:].reshape(bp, tp, H_FOX, HEAD_DIM),
                       zg3[:, :, G_FF:G_FF + H_FOX],
                       lat.reshape(bp, tp, -1)) + _state_out(*st))

        c0 = _block_diag_state(state_mlstm_C[l])
        n0 = state_mlstm_n[l].reshape(bs, 1, wm)
        m0 = jnp.broadcast_to(jnp.pad(state_mlstm_m[l], ((0, 0), (0, SUBLANES - H_MLSTM)))[:, :, None],
                              (bs, SUBLANES, LANES))
        zf, zg3, gt3, hm, st, qa, lat, latp = _layer_common(ys, lw, cs_s, sn_s, c0, n0, m0, ts, 1, bs * PAD_T, F32)
        zf3 = zf.reshape(bs, PAD_T, -1)
        valid = (jnp.arange(PAD_T) < ts)[None, :, None]

        def new_page(a):
            a = jnp.swapaxes(jnp.where(valid, a, 0.0), 1, 2)
            return jnp.pad(a, ((0, 0), (0, 0), (0, page - PAD_T)))

        new_pages = (new_page(zf3[:, :, wf:2 * wf]), new_page(zf3[:, :, 2 * wf:]),
                     new_page(lat.reshape(bs, PAD_T, -1)), new_page(zg3[:, :, G_FF:G_FF + H_FOX]))
        qa4 = jnp.swapaxes(qa.reshape(H_MLA, bs, PAD_T, 2 * LANES), 0, 1)
        hf, ha = _decode(page_table, l, pools, new_pages, zf3[:, :, :wf], qa4, lw["wuv_pad"], lw["gh_f"], lw["gh_a"])
        ys = _out_ffn(ys.reshape(bs * PAD_T, d), hm.reshape(bs * PAD_T, -1), hf.reshape(bs * PAD_T, -1),
                      ha.reshape(bs * PAD_T, -1), lw, alpha).reshape(bs, PAD_T, d)
        outs_s.append((zf3[:, :ts, wf:2 * wf].reshape(bs, ts, H_FOX, HEAD_DIM),
                       zf3[:, :ts, 2 * wf:].reshape(bs, ts, H_FOX, HEAD_DIM),
                       zg3[:, :ts, G_FF:G_FF + H_FOX],
                       lat.reshape(bs, PAD_T, -1)[:, :ts]) + _state_out(*st))

    sp = [jnp.stack(a) for a in zip(*outs_p)]
    ss = [jnp.stack(a) for a in zip(*outs_s)]
    return (yp, ys[:, :ts], sp[0], sp[1], sp[2], sp[3], sp[4], sp[5], sp[6],
            ss[0], ss[1], ss[2], ss[3], ss[4], ss[5], ss[6])
```

```python
import functools

import jax
import jax.numpy as jnp
import numpy as np
from jax import lax
from jax.experimental import pallas as pl
from jax.experimental.pallas import tpu as pltpu

F32 = jnp.float32
BF16 = jnp.bfloat16

HEAD_DIM = 64
H_MLSTM = 4
H_FOX = 4
H_MLA = 8
MLSTM_CHUNK = 128
KV_LORA = 128
NOPE_DIM = 64
ROPE_DIM = 32
ROPE_THETA = 10000.0
LN_EPS = 1e-5
RMS_EPS = 1e-6

LANES = 128
SUBLANES = 8
VMEM_LIMIT = 56 * 1024 * 1024

C_M = 0
C_F = 1024
C_CQ = 1792
C_CKV = 2048
C_G = 2176
D_INP = 2304
G_KR = 0
G_MI = 32
G_MF = 36
G_FF = 40
G_ROWS = 16

PAD_T = 8
NEG_INF = float("-inf")
LOG2E = 1.4426950408889634
STAGE_LAG = 2
ATT_TQ = 256
ATT_TK = 512


def _bf(x):
    return x.astype(BF16)


def _dot(a, b):
    return jnp.dot(a, b, preferred_element_type=F32)


def _dot_nt(a, b):
    return lax.dot_general(a, b, (((1,), (1,)), ((), ())), preferred_element_type=F32)


def _dot_tn(a, b):
    return lax.dot_general(a, b, (((0,), (0,)), ((), ())), preferred_element_type=F32)


def _split3(x):
    hi = _bf(x)
    r = x - hi.astype(F32)
    mid = _bf(r)
    lo = _bf(r - mid.astype(F32))
    return hi, mid, lo


def _cumsum_rows(tril_bf, x):
    hi, mid, lo = _split3(x)
    return _dot(tril_bf, hi) + _dot(tril_bf, mid) + _dot(tril_bf, lo)


def _cumsum_lanes(x, triu_bf):
    hi, mid, lo = _split3(x)
    return _dot(hi, triu_bf) + _dot(mid, triu_bf) + _dot(lo, triu_bf)


def _tri(n, lower):
    r = lax.broadcasted_iota(jnp.int32, (n, n), 0)
    c = lax.broadcasted_iota(jnp.int32, (n, n), 1)
    return jnp.where((r >= c) if lower else (r <= c), 1.0, 0.0).astype(BF16)


def _log_sigmoid(x):
    return jnp.minimum(x, 0.0) - jnp.log1p(jnp.exp(-jnp.abs(x)))


def _head_mask(width, h):
    lane = lax.broadcasted_iota(jnp.int32, (1, width), 1)
    return (lane // HEAD_DIM) == h


def _head_rmsnorm(x, gh, n_heads):
    width = n_heads * HEAD_DIM
    x2 = x * x
    inv = jnp.zeros_like(x)
    for h in range(n_heads):
        mk = _head_mask(width, h)
        ms = jnp.sum(jnp.where(mk, x2, 0.0), axis=1, keepdims=True) * (1.0 / HEAD_DIM)
        inv = jnp.where(mk, lax.rsqrt(ms + RMS_EPS), inv)
    return x * inv * gh


def _layernorm(x, g, b):
    mu = jnp.mean(x, axis=1, keepdims=True)
    xc = x - mu
    var = jnp.mean(xc * xc, axis=1, keepdims=True)
    return xc * lax.rsqrt(var + LN_EPS) * g + b


def _params(sem):
    return pltpu.CompilerParams(dimension_semantics=sem, vmem_limit_bytes=VMEM_LIMIT)


def _token_tile(n):
    for t in (512, 256, 128, 64, 32, 16, 8):
        if n % t == 0:
            return t
    raise ValueError(f"token count {n} is not a multiple of 8")


def _const_spec(shape):
    nd = len(shape)
    return pl.BlockSpec(shape, lambda *_: (0,) * nd)


def _inproj_kernel(x_ref, w_ref, gb_ref, zm_ref, zf_ref, zcq_ref, zckv_ref, zg_ref, kb_ref, vb_ref):
    xb = _bf(x_ref[...])
    for c in range(4):
        zm_ref[:, c * 256:(c + 1) * 256] = _dot(xb, w_ref[:, C_M + c * 256:C_M + (c + 1) * 256])
    for c in range(3):
        z = _dot(xb, w_ref[:, C_F + c * 256:C_F + (c + 1) * 256])
        zf_ref[:, c * 256:(c + 1) * 256] = z
        if c == 1:
            kb_ref[...] = _bf(z)
        if c == 2:
            vb_ref[...] = _bf(z)
    zcq_ref[...] = _dot(xb, w_ref[:, C_CQ:C_CQ + 256])
    last = _dot(xb, w_ref[:, C_CKV:D_INP])
    zckv_ref[...] = last[:, :KV_LORA]
    g = last[:, KV_LORA:] + gb_ref[...]
    lane = lax.broadcasted_iota(jnp.int32, g.shape, 1)
    zg_ref[...] = jnp.where((lane >= G_MF) & (lane < G_FF + H_FOX), _log_sigmoid(g), g)


def _inproj(x, w_perm, gbias):
    n, d = x.shape
    tm = _token_tile(n)
    row = lambda w: pl.BlockSpec((tm, w), lambda i: (i, 0))
    wf = H_FOX * HEAD_DIM
    widths = (1024, 768, 256, KV_LORA, LANES, wf, wf)
    dtypes = (F32,) * 5 + (BF16,) * 2
    return pl.pallas_call(
        _inproj_kernel,
        grid=(n // tm,),
        in_specs=[row(d), _const_spec(w_perm.shape), _const_spec(gbias.shape)],
        out_specs=[row(w) for w in widths],
        out_shape=[jax.ShapeDtypeStruct((n, w), dt) for w, dt in zip(widths, dtypes)],
        compiler_params=_params(("parallel",)),
        name="inproj",
    )(x, w_perm, gbias)


def _mlstm_kernel(q_ref, k_ref, v_ref, o_ref, zg_ref, gt_ref, c0_ref, n0_ref, m0_ref, gh_ref,
                  hm_ref, c1_ref, n1_ref, m1_ref, c_sc, n_sc, m_sc, *, chunk, n_valid):
    L = chunk
    W = H_MLSTM * HEAD_DIM
    ci = pl.program_id(1)

    @pl.when(ci == 0)
    def _():
        c_sc[...] = c0_ref[0]
        n_sc[...] = n0_ref[0]
        m_sc[...] = m0_ref[0]

    q = q_ref[0]
    k = k_ref[0] * (HEAD_DIM ** -0.5)
    v = v_ref[0]
    zg = zg_ref[0]
    gt = gt_ref[0]
    valid_c = lax.broadcasted_iota(jnp.int32, (L, 1), 0) < n_valid
    valid_r = lax.broadcasted_iota(jnp.int32, (1, L), 1) < n_valid
    cum_c = _cumsum_rows(_tri(L, True), jnp.where(valid_c, zg, 0.0))
    cum_r = _cumsum_lanes(jnp.where(valid_r, gt, 0.0), _tri(L, False))
    row = lax.broadcasted_iota(jnp.int32, (L, L), 0)
    col = lax.broadcasted_iota(jnp.int32, (L, L), 1)
    tril = row >= col
    qb, kb, vb = _bf(q), _bf(k), _bf(v)
    cb = _bf(c_sc[...])
    n_prev = n_sc[...]

    h_all = jnp.zeros((L, W), F32)
    w_all = jnp.zeros((L, W), F32)
    cdec_all = jnp.zeros((1, W), F32)
    for h in range(H_MLSTM):
        mk = _head_mask(W, h)
        b_c = cum_c[:, G_MF + h:G_MF + h + 1]
        b_r = cum_r[H_MLSTM + h:H_MLSTM + h + 1, :]
        ig_c = jnp.where(valid_c, zg[:, G_MI + h:G_MI + h + 1], NEG_INF)
        ig_r = jnp.where(valid_r, gt[h:h + 1, :], NEG_INF)
        m_prev = m_sc[h:h + 1, 0:1]
        dlog = jnp.where(tril, b_c - b_r + ig_r, NEG_INF)
        inter = b_c + m_prev
        mt = jnp.maximum(inter, jnp.max(dlog, axis=1, keepdims=True))
        dmat = jnp.exp(dlog - mt)
        qh = jnp.where(mk, q, 0.0)
        qhb = _bf(qh)
        s = _dot_nt(qhb, kb) * dmat
        dec = jnp.exp(inter - mt)
        num = _dot(_bf(s), vb) + dec * _dot(qhb, cb)
        den = jnp.sum(s, axis=1, keepdims=True) + dec * jnp.sum(qh * n_prev, axis=1, keepdims=True)
        hh = num / jnp.maximum(jnp.abs(den), jnp.exp(-mt))
        h_all = jnp.where(mk, hh, h_all)
        m_new = mt[L - 1:L, :]
        b_last = b_c[L - 1:L, :]
        w_c = jnp.exp(b_last - b_c + ig_c - m_new)
        cdec = jnp.exp(b_last + m_prev - m_new)
        w_all = jnp.where(mk, w_c, w_all)
        cdec_all = jnp.where(mk, cdec, cdec_all)
        m_sc[h:h + 1, :] = jnp.broadcast_to(m_new, (1, LANES))

    kw = k * w_all
    r2 = lax.broadcasted_iota(jnp.int32, (W, W), 0) // HEAD_DIM
    c2 = lax.broadcasted_iota(jnp.int32, (W, W), 1) // HEAD_DIM
    c_sc[...] = jnp.where(r2 == c2, cdec_all * c_sc[...] + _dot_tn(_bf(kw), vb), 0.0)
    n_sc[...] = cdec_all * n_prev + jnp.sum(kw, axis=0, keepdims=True)

    hm_ref[0] = _head_rmsnorm(h_all, gh_ref[...], H_MLSTM) * jax.nn.sigmoid(o_ref[0])

    @pl.when(ci == pl.num_programs(1) - 1)
    def _():
        c1_ref[0] = c_sc[...]
        n1_ref[0] = n_sc[...]
        m1_ref[0] = m_sc[...]


def _mlstm(zm3, zg3, gt3, c0, n0, m0, gh, n_valid):
    b, t, _ = zm3.shape
    chunk = MLSTM_CHUNK if t % MLSTM_CHUNK == 0 else t
    nc = t // chunk
    w = H_MLSTM * HEAD_DIM
    col = lambda j: pl.BlockSpec((1, chunk, w), lambda bi, ci: (bi, ci, j))
    state = lambda shape: pl.BlockSpec((1,) + shape, lambda bi, ci: (bi, 0, 0))
    return pl.pallas_call(
        functools.partial(_mlstm_kernel, chunk=chunk, n_valid=n_valid),
        grid=(b, nc),
        in_specs=[col(0), col(1), col(2), col(3),
                  pl.BlockSpec((1, chunk, LANES), lambda bi, ci: (bi, ci, 0)),
                  pl.BlockSpec((1, G_ROWS, chunk), lambda bi, ci: (bi, 0, ci)),
                  state((w, w)), state((1, w)), state((SUBLANES, LANES)),
                  _const_spec((1, w))],
        out_specs=[pl.BlockSpec((1, chunk, w), lambda bi, ci: (bi, ci, 0)),
                   state((w, w)), state((1, w)), state((SUBLANES, LANES))],
        out_shape=[jax.ShapeDtypeStruct((b, t, w), F32),
                   jax.ShapeDtypeStruct((b, w, w), F32),
                   jax.ShapeDtypeStruct((b, 1, w), F32),
                   jax.ShapeDtypeStruct((b, SUBLANES, LANES), F32)],
        scratch_shapes=[pltpu.VMEM((w, w), F32), pltpu.VMEM((1, w), F32), pltpu.VMEM((SUBLANES, LANES), F32)],
        compiler_params=_params(("parallel", "arbitrary")),
        name="mlstm",
    )(zm3, zm3, zm3, zm3, zg3, gt3, c0, n0, m0, gh)


def _fcum_kernel(gt_ref, fr_ref, cr_sc, *, chunk):
    L = chunk

    @pl.when(pl.program_id(1) == 0)
    def _():
        cr_sc[...] = jnp.zeros_like(cr_sc)

    cum_r = _cumsum_lanes(gt_ref[0], _tri(L, False)) + cr_sc[:, 0:1]
    fr_ref[0] = cum_r
    cr_sc[...] = jnp.broadcast_to(cum_r[:, L - 1:L], cr_sc.shape)


def _fcum(gt3):
    b, _, t = gt3.shape
    chunk = 512 if t % 512 == 0 else t
    return pl.pallas_call(
        functools.partial(_fcum_kernel, chunk=chunk),
        grid=(b, t // chunk),
        in_specs=[pl.BlockSpec((1, G_ROWS, chunk), lambda bi, ci: (bi, 0, ci))],
        out_specs=pl.BlockSpec((1, G_ROWS, chunk), lambda bi, ci: (bi, 0, ci)),
        out_shape=jax.ShapeDtypeStruct(gt3.shape, F32),
        scratch_shapes=[pltpu.VMEM((G_ROWS, LANES), F32)],
        compiler_params=_params(("parallel", "arbitrary")),
        name="fcum",
    )(gt3)


def _lane_tile(x, n):
    return x if n == 1 else jnp.concatenate([x] * n, axis=1)


def _staged(n, lag, *stages):
    for i in range(n + lag * (len(stages) - 1)):
        for s, stage in enumerate(stages):
            if 0 <= i - s * lag < n:
                stage(i - s * lag)


def _fox_kernel(q_ref, k_ref, v_ref, fr_ref, gh_ref, o_ref, qh_sc, acc_sc, m_sc, l_sc, *stage_sc, tq, tk):
    s_sc, p_sc, al_sc = stage_sc[0:H_FOX], stage_sc[H_FOX:2 * H_FOX], stage_sc[2 * H_FOX:]
    W = H_FOX * HEAD_DIM
    nt = tk // LANES
    qi = pl.program_id(1)
    n_full = (qi * tq) // tk
    q = q_ref[0]
    for h in range(H_FOX):
        qh_sc[h] = _bf(jnp.where(_head_mask(W, h), q, 0.0))
    acc_sc[...] = jnp.zeros_like(acc_sc)
    m_sc[...] = jnp.full_like(m_sc, NEG_INF)
    l_sc[...] = jnp.zeros_like(l_sc)

    def step(j, masked):
        off = pl.multiple_of(j * tk, tk)
        kb = k_ref[0, pl.ds(off, tk), :]
        vb = v_ref[0, pl.ds(off, tk), :]
        fk = fr_ref[0, :, pl.ds(off, tk)] * LOG2E
        if masked:
            qpos = qi * tq + lax.broadcasted_iota(jnp.int32, (tq, tk), 0)
            kpos = off + lax.broadcasted_iota(jnp.int32, (tq, tk), 1)
            causal = kpos <= qpos

        def scores(h):
            s_sc[h][...] = _dot_nt(qh_sc[h], kb)

        def softmax(h):
            t = s_sc[h][...] * (HEAD_DIM ** -0.5 * LOG2E) - fk[2 * H_MLSTM + h:2 * H_MLSTM + h + 1, :]
            if masked:
                t = jnp.where(causal, t, NEG_INF)
            m_prev = m_sc[h]
            m_new = jnp.maximum(m_prev, jnp.max(t, axis=1, keepdims=True))
            alpha = jnp.exp2(m_prev - m_new)
            p = jnp.exp2(t - _lane_tile(m_new, nt))
            l_sc[h] = alpha * l_sc[h] + jnp.sum(p, axis=1, keepdims=True)
            m_sc[h] = m_new
            al_sc[h][...] = alpha
            p_sc[h][...] = _bf(p)

        def values(h):
            acc_sc[h] = _lane_tile(al_sc[h][...], W // LANES) * acc_sc[h] + _dot(p_sc[h][...], vb)

        _staged(H_FOX, STAGE_LAG, scores, softmax, values)

    def full_step(j, carry):
        step(j, False)
        return carry

    lax.fori_loop(0, n_full, full_step, 0)
    step(n_full, True)

    out = jnp.zeros((tq, W), F32)
    for h in range(H_FOX):
        out = jnp.where(_head_mask(W, h), acc_sc[h] * _lane_tile(1.0 / l_sc[h], W // LANES), out)
    o_ref[0] = _head_rmsnorm(out, gh_ref[...], H_FOX)


def _fox_prompt(zf3, kb3, vb3, fr3, gh):
    b, t, _ = zf3.shape
    w = H_FOX * HEAD_DIM
    tq = ATT_TQ if t % ATT_TQ == 0 else t
    tk = ATT_TK if t % ATT_TK == 0 else t
    seq = lambda shape: pl.BlockSpec((1,) + shape, lambda bi, qi: (bi, 0, 0))
    return pl.pallas_call(
        functools.partial(_fox_kernel, tq=tq, tk=tk),
        grid=(b, t // tq),
        in_specs=[pl.BlockSpec((1, tq, w), lambda bi, qi: (bi, qi, 0)),
                  seq((t, w)), seq((t, w)), seq((G_ROWS, t)), _const_spec((1, w))],
        out_specs=pl.BlockSpec((1, tq, w), lambda bi, qi: (bi, qi, 0)),
        out_shape=jax.ShapeDtypeStruct((b, t, w), F32),
        scratch_shapes=([pltpu.VMEM((H_FOX, tq, w), BF16), pltpu.VMEM((H_FOX, tq, w), F32),
                         pltpu.VMEM((H_FOX, tq, LANES), F32), pltpu.VMEM((H_FOX, tq, LANES), F32)]
                        + [pltpu.VMEM((tq, tk), F32)] * H_FOX + [pltpu.VMEM((tq, tk), BF16)] * H_FOX
                        + [pltpu.VMEM((tq, LANES), F32)] * H_FOX),
        compiler_params=_params(("parallel", "arbitrary")),
        name="fox_prompt",
    )(zf3, kb3, vb3, fr3, gh)


def _rope_swap(x, group):
    w = x.shape[1]
    half = ROPE_DIM // 2
    lane = lax.broadcasted_iota(jnp.int32, x.shape, 1)
    first = (lane % group) < half
    return jnp.where(first, pltpu.roll(x, w - half, 1), pltpu.roll(x, half, 1))


def _mla_pre_kernel(zcq_ref, zckv_ref, zg_ref, cs_ref, sn_ref, gcq_ref, wn_ref, wr_ref, gckv_ref, wuk_ref, sel_ref,
                    q_ref, lat_ref, latp_ref):
    zcq = zcq_ref[...]
    cqn = zcq * lax.rsqrt(jnp.mean(zcq * zcq, axis=1, keepdims=True) + RMS_EPS) * gcq_ref[...]
    cqb = _bf(cqn)
    qn = _dot(cqb, wn_ref[...])
    qr = _dot(cqb, wr_ref[...])
    cs = cs_ref[...]
    sn = sn_ref[...]
    qrot = qr * cs + _rope_swap(qr, ROPE_DIM) * sn
    qlat = _dot(_bf(qn), wuk_ref[...])
    qrope = _dot(_bf(qrot), sel_ref[...])
    for h in range(H_MLA):
        q_ref[0, h, :, 0:LANES] = qlat[:, h * LANES:(h + 1) * LANES].astype(q_ref.dtype)
        q_ref[0, h, :, LANES:2 * LANES] = qrope[:, h * LANES:(h + 1) * LANES].astype(q_ref.dtype)

    zckv = zckv_ref[...]
    ckvn = zckv * lax.rsqrt(jnp.mean(zckv * zckv, axis=1, keepdims=True) + RMS_EPS) * gckv_ref[...]
    zg = zg_ref[...]
    krot = zg * cs[:, :LANES] + _rope_swap(zg, ROPE_DIM) * sn[:, :LANES]
    lane = lax.broadcasted_iota(jnp.int32, krot.shape, 1)
    krz = jnp.where(lane < ROPE_DIM, krot, jnp.where(lane == LANES - 1, 1.0, 0.0))
    lat_ref[:, 0:KV_LORA] = ckvn
    lat_ref[:, KV_LORA:KV_LORA + ROPE_DIM] = krot[:, 0:ROPE_DIM]
    latp_ref[:, 0:KV_LORA] = _bf(ckvn)
    latp_ref[:, KV_LORA:2 * KV_LORA] = _bf(krz)


def _mla_pre(zcq, zckv, zg, cs, sn, lw, bq, tq, q_dtype):
    n = zcq.shape[0]
    tm = _token_tile(min(n, tq))
    tm = min(tm, tq)
    nt_tab = cs.shape[0] // tm
    nt_q = tq // tm
    row = lambda w: pl.BlockSpec((tm, w), lambda i: (i, 0))
    tab = pl.BlockSpec((tm, 2 * LANES), lambda i: (i % nt_tab, 0))
    consts = (lw["g_cq"], lw["w_uq_n"], lw["w_uq_r"], lw["g_ckv"], lw["wuk_bd"], lw["sel"])
    return pl.pallas_call(
        _mla_pre_kernel,
        grid=(n // tm,),
        in_specs=[row(256), row(KV_LORA), row(LANES), tab, tab] + [_const_spec(c.shape) for c in consts],
        out_specs=[pl.BlockSpec((1, H_MLA, tm, 2 * LANES), lambda i: (i // nt_q, 0, i % nt_q, 0)),
                   row(KV_LORA + ROPE_DIM), row(2 * LANES)],
        out_shape=[jax.ShapeDtypeStruct((bq, H_MLA, tq, 2 * LANES), q_dtype),
                   jax.ShapeDtypeStruct((n, KV_LORA + ROPE_DIM), F32),
                   jax.ShapeDtypeStruct((n, 2 * LANES), BF16)],
        compiler_params=_params(("parallel",)),
        name="mla_pre",
    )(zcq, zckv, zg, cs, sn, *consts)


def _mla_out(acc, inv_l, wuv_ref, gh, rows):
    out = jnp.zeros((rows, H_MLA * HEAD_DIM), F32)
    for h in range(H_MLA):
        o_h = acc[h * rows:(h + 1) * rows] * inv_l[h * rows:(h + 1) * rows]
        out = out + _dot(_bf(o_h), wuv_ref[h])
    return _head_rmsnorm(out, gh, H_MLA)


def _mla_kernel(q_ref, k_ref, wuv_ref, gh_ref, o_ref, acc_sc, m_sc, *stage_sc, tq, tk):
    s_sc, p_sc, al_sc = stage_sc[0:H_MLA], stage_sc[H_MLA:2 * H_MLA], stage_sc[2 * H_MLA:]
    nt = tk // LANES
    scale2 = (NOPE_DIM + ROPE_DIM) ** -0.5 * LOG2E
    qi = pl.program_id(1)
    n_full = (qi * tq) // tk
    acc_sc[...] = jnp.zeros_like(acc_sc)
    m_sc[...] = jnp.full_like(m_sc, NEG_INF)

    def step(j, masked):
        off = pl.multiple_of(j * tk, tk)
        kb = k_ref[0, pl.ds(off, tk), :]
        if masked:
            qpos = qi * tq + lax.broadcasted_iota(jnp.int32, (tq, tk), 0)
            kpos = off + lax.broadcasted_iota(jnp.int32, (tq, tk), 1)
            causal = kpos <= qpos
        def scores(h):
            s_sc[h][...] = _dot_nt(q_ref[0, h], kb)

        def softmax(h):
            s = s_sc[h][...]
            if masked:
                s = jnp.where(causal, s, NEG_INF)
            m_prev = m_sc[h]
            m_new = jnp.maximum(m_prev, jnp.max(s, axis=1, keepdims=True))
            al_sc[h][...] = jnp.exp2((m_prev - m_new) * scale2)
            p_sc[h][...] = _bf(jnp.exp2((s - _lane_tile(m_new, nt)) * scale2))
            m_sc[h] = m_new

        def values(h):
            acc_sc[h] = _lane_tile(al_sc[h][...], 2) * acc_sc[h] + _dot(p_sc[h][...], kb)

        _staged(H_MLA, STAGE_LAG, scores, softmax, values)

    def full_step(j, carry):
        step(j, False)
        return carry

    lax.fori_loop(0, n_full, full_step, 0)
    step(n_full, True)

    out = jnp.zeros((tq, H_MLA * HEAD_DIM), F32)
    for h in range(H_MLA):
        acc = acc_sc[h]
        o_h = acc[:, :KV_LORA] * (1.0 / acc[:, 2 * LANES - 1:2 * LANES])
        out = out + _dot(_bf(o_h), wuv_ref[h])
    o_ref[0] = _head_rmsnorm(out, gh_ref[...], H_MLA)


def _mla_prompt(qp, latp3, wuv_pad, gh):
    b, _, t, _ = qp.shape
    tq = ATT_TQ if t % ATT_TQ == 0 else t
    tk = ATT_TK if t % ATT_TK == 0 else t
    w = H_MLA * HEAD_DIM
    return pl.pallas_call(
        functools.partial(_mla_kernel, tq=tq, tk=tk),
        grid=(b, t // tq),
        in_specs=[pl.BlockSpec((1, H_MLA, tq, 2 * LANES), lambda bi, qi: (bi, 0, qi, 0)),
                  pl.BlockSpec((1, t, 2 * LANES), lambda bi, qi: (bi, 0, 0)),
                  _const_spec(wuv_pad.shape), _const_spec((1, w))],
        out_specs=pl.BlockSpec((1, tq, w), lambda bi, qi: (bi, qi, 0)),
        out_shape=jax.ShapeDtypeStruct((b, t, w), F32),
        scratch_shapes=([pltpu.VMEM((H_MLA, tq, 2 * LANES), F32), pltpu.VMEM((H_MLA, tq, LANES), F32)]
                        + [pltpu.VMEM((tq, tk), F32)] * H_MLA + [pltpu.VMEM((tq, tk), BF16)] * H_MLA
                        + [pltpu.VMEM((tq, LANES), F32)] * H_MLA),
        compiler_params=_params(("parallel", "arbitrary")),
        name="mla_prompt",
    )(qp, latp3, wuv_pad, gh)


def _decode_kernel(pt_ref, *refs, n_group):
    G = n_group
    kt_refs = refs[0:G]
    vt_refs = refs[G:2 * G]
    lt_refs = refs[2 * G:3 * G]
    lf_refs = refs[3 * G:4 * G]
    (ktn_ref, vtn_ref, ltn_ref, lfn_ref, qf_ref, qa_ref, wuv_ref, ghf_ref, gha_ref,
     of_ref, oa_ref,
     facc, fm, fl, aacc, am, al, carry, kbuf, lfbuf) = refs[4 * G:]
    del pt_ref
    j = pl.program_id(1)
    nj = pl.num_programs(1)
    WF = H_FOX * HEAD_DIM
    RF = H_FOX * PAD_T
    RA = H_MLA * PAD_T

    @pl.when(j == 0)
    def _():
        facc[...] = jnp.zeros_like(facc)
        fm[...] = jnp.full_like(fm, NEG_INF)
        fl[...] = jnp.zeros_like(fl)
        aacc[...] = jnp.zeros_like(aacc)
        am[...] = jnp.full_like(am, NEG_INF)
        al[...] = jnp.zeros_like(al)
        carry[...] = jnp.zeros_like(carry)
        kbuf[...] = jnp.zeros_like(kbuf)
        lfbuf[...] = jnp.zeros_like(lfbuf)

    q = qf_ref[0]
    qbd = jnp.concatenate([_bf(jnp.where(_head_mask(WF, h), q, 0.0)) for h in range(H_FOX)], axis=0)
    qa = _bf(qa_ref[0].reshape(RA, 2 * LANES))
    triu = _tri(LANES, False)
    scale_f = HEAD_DIM ** -0.5 * LOG2E
    scale_a = (NOPE_DIM + ROPE_DIM) ** -0.5 * LOG2E

    def block(kts, vts, lts, lfs, new_tokens):
        n = len(kts)
        for g in range(n):
            lfbuf[g * SUBLANES:g * SUBLANES + H_FOX, :] = lfs[g][...]
        cs = _cumsum_lanes(lfbuf[0:n * SUBLANES, :], triu)
        tot = jnp.broadcast_to(cs[:, LANES - 1:LANES], cs.shape)
        base = jnp.concatenate([carry[...]] * n, axis=0)
        if n > 1:
            r = lax.broadcasted_iota(jnp.int32, (n * SUBLANES, n * SUBLANES), 0)
            c = lax.broadcasted_iota(jnp.int32, (n * SUBLANES, n * SUBLANES), 1)
            earlier = jnp.where((r % SUBLANES == c % SUBLANES) & (c // SUBLANES < r // SUBLANES), 1.0, 0.0)
            base = base + _cumsum_rows(earlier.astype(BF16), tot)
        if not new_tokens:
            last = slice((n - 1) * SUBLANES, n * SUBLANES)
            carry[...] = base[last] + tot[last]
        cum = (cs + base) * LOG2E
        s_f, s_a = [], []
        for g in range(n):
            bias = jnp.concatenate([jnp.broadcast_to(cum[g * SUBLANES + h:g * SUBLANES + h + 1, :], (PAD_T, LANES))
                                    for h in range(H_FOX)], axis=0)
            s_f.append(_dot(qbd, _bf(kts[g][...])) * scale_f - bias)
            kbuf[g, 0:KV_LORA + ROPE_DIM, :] = _bf(lts[g][...])
            s_a.append(_dot(qa, kbuf[g]))
        s_f = jnp.concatenate(s_f, axis=1)
        s_a = jnp.concatenate(s_a, axis=1)
        if new_tokens:
            ok_f = lax.broadcasted_iota(jnp.int32, s_f.shape, 1) <= lax.broadcasted_iota(jnp.int32, s_f.shape, 0) % PAD_T
            ok_a = lax.broadcasted_iota(jnp.int32, s_a.shape, 1) <= lax.broadcasted_iota(jnp.int32, s_a.shape, 0) % PAD_T
            s_f = jnp.where(ok_f, s_f, NEG_INF)
            s_a = jnp.where(ok_a, s_a, NEG_INF)
        m_prev = fm[...]
        m_new = jnp.maximum(m_prev, jnp.max(s_f, axis=1, keepdims=True))
        alpha = jnp.exp2(m_prev - m_new)
        p = jnp.exp2(s_f - _lane_tile(m_new, n))
        fl[...] = alpha * fl[...] + jnp.sum(p, axis=1, keepdims=True)
        fm[...] = m_new
        pv = jnp.zeros((RF, WF), F32)
        for g in range(n):
            pv = pv + _dot_nt(_bf(p[:, g * LANES:(g + 1) * LANES]), _bf(vts[g][...]))
        facc[...] = _lane_tile(alpha, WF // LANES) * facc[...] + pv
        m_prev = am[...]
        m_new = jnp.maximum(m_prev, jnp.max(s_a, axis=1, keepdims=True))
        alpha = jnp.exp2((m_prev - m_new) * scale_a)
        p = jnp.exp2((s_a - _lane_tile(m_new, n)) * scale_a)
        al[...] = alpha * al[...] + jnp.sum(p, axis=1, keepdims=True)
        am[...] = m_new
        pv = jnp.zeros((RA, KV_LORA), F32)
        for g in range(n):
            pv = pv + _dot_nt(_bf(p[:, g * LANES:(g + 1) * LANES]), _bf(lts[g][0:KV_LORA, :]))
        aacc[...] = alpha * aacc[...] + pv

    block([r.at[0, 0] for r in kt_refs], [r.at[0, 0] for r in vt_refs],
          [r.at[0, 0] for r in lt_refs], [r.at[0, 0] for r in lf_refs], False)

    @pl.when(j == nj - 1)
    def _():
        block([ktn_ref.at[0]], [vtn_ref.at[0]], [ltn_ref.at[0]], [lfn_ref.at[0]], True)
        accf = facc[...] * _lane_tile(1.0 / fl[...], WF // LANES)
        of = jnp.zeros((PAD_T, WF), F32)
        for h in range(H_FOX):
            of = jnp.where(_head_mask(WF, h), accf[h * PAD_T:(h + 1) * PAD_T], of)
        of_ref[0] = _head_rmsnorm(of, ghf_ref[...], H_FOX)
        oa_ref[0] = _mla_out(aacc[...], 1.0 / al[...], wuv_ref, gha_ref[...], PAD_T)


def _decode(page_table, layer, caches, new_pages, qf, qa, wuv_pad, ghf, gha):
    b, n_pages = page_table.shape
    G = 16
    while n_pages % G:
        G //= 2
    kt, vt, lt, lf = caches
    page = kt.shape[-1]

    def pool_spec(arr, g):
        rows = arr.shape[2]
        return pl.BlockSpec((1, 1, rows, page), lambda bi, ji, pt: (layer, pt[bi, ji * G + g], 0, 0))

    def seq_spec(arr):
        nd = arr.ndim
        return pl.BlockSpec((1,) + arr.shape[1:], lambda bi, ji, pt: (bi,) + (0,) * (nd - 1))

    def const(arr):
        nd = arr.ndim
        return pl.BlockSpec(arr.shape, lambda bi, ji, pt: (0,) * nd)

    in_specs = []
    operands = []
    for arr in (kt, vt, lt, lf):
        for g in range(G):
            in_specs.append(pool_spec(arr, g))
            operands.append(arr)
    for arr in tuple(new_pages) + (qf, qa):
        in_specs.append(seq_spec(arr))
        operands.append(arr)
    for arr in (wuv_pad, ghf, gha):
        in_specs.append(const(arr))
        operands.append(arr)
    wf = H_FOX * HEAD_DIM
    wa = H_MLA * HEAD_DIM
    grid_spec = pltpu.PrefetchScalarGridSpec(
        num_scalar_prefetch=1,
        grid=(b, n_pages // G),
        in_specs=in_specs,
        out_specs=[pl.BlockSpec((1, PAD_T, wf), lambda bi, ji, pt: (bi, 0, 0)),
                   pl.BlockSpec((1, PAD_T, wa), lambda bi, ji, pt: (bi, 0, 0))],
        scratch_shapes=[pltpu.VMEM((H_FOX * PAD_T, wf), F32), pltpu.VMEM((H_FOX * PAD_T, LANES), F32),
                        pltpu.VMEM((H_FOX * PAD_T, LANES), F32),
                        pltpu.VMEM((H_MLA * PAD_T, KV_LORA), F32), pltpu.VMEM((H_MLA * PAD_T, LANES), F32),
                        pltpu.VMEM((H_MLA * PAD_T, LANES), F32),
                        pltpu.VMEM((SUBLANES, LANES), F32), pltpu.VMEM((G, 2 * LANES, page), BF16),
                        pltpu.VMEM((G * SUBLANES, LANES), F32)])
    return pl.pallas_call(
        functools.partial(_decode_kernel, n_group=G),
        grid_spec=grid_spec,
        out_shape=[jax.ShapeDtypeStruct((b, PAD_T, wf), F32), jax.ShapeDtypeStruct((b, PAD_T, wa), F32)],
        compiler_params=_params(("parallel", "arbitrary")),
        name="decode",
    )(page_table, *operands)


def _out_ffn_kernel(x_ref, hm_ref, hf_ref, ha_ref, wom_ref, wof_ref, woa_ref, g1_ref, b1_ref,
                    wg_ref, wu_ref, wd_ref, g2_ref, b2_ref, y_ref, acc_sc, *, alpha, n_chunks):
    mix = (_dot(_bf(hm_ref[...]), wom_ref[...]) + _dot(_bf(hf_ref[...]), wof_ref[...])
           + _dot(_bf(ha_ref[...]), woa_ref[...]))
    x1 = _layernorm(alpha * x_ref[...] + mix, g1_ref[...], b1_ref[...])
    x1b = _bf(x1)
    acc_sc[...] = jnp.zeros_like(acc_sc)

    def body(c, carry):
        g = _dot(x1b, wg_ref[c])
        u = _dot(x1b, wu_ref[c])
        acc_sc[...] += _dot(_bf(g * jax.nn.sigmoid(g) * u), wd_ref[c])
        return carry

    lax.fori_loop(0, n_chunks, body, 0)
    y_ref[...] = _layernorm(alpha * x1 + acc_sc[...], g2_ref[...], b2_ref[...])


def _out_ffn(x, hm, hf, ha, lw, alpha):
    n, d = x.shape
    tm = _token_tile(n)
    row = lambda w: pl.BlockSpec((tm, w), lambda i: (i, 0))
    consts = (lw["wo_m"], lw["wo_f"], lw["wo_a"], lw["ln1_g"], lw["ln1_b"],
              lw["wg"], lw["wu"], lw["wd"], lw["ln2_g"], lw["ln2_b"])

    def resident(c):
        nd = c.ndim
        return pl.BlockSpec(c.shape, lambda i: (0,) * nd, pipeline_mode=pl.Buffered(1))

    return pl.pallas_call(
        functools.partial(_out_ffn_kernel, alpha=alpha, n_chunks=lw["wg"].shape[0]),
        grid=(n // tm,),
        in_specs=[row(d), row(hm.shape[1]), row(hf.shape[1]), row(ha.shape[1])] + [resident(c) for c in consts],
        out_specs=row(d),
        out_shape=jax.ShapeDtypeStruct((n, d), F32),
        scratch_shapes=[pltpu.VMEM((tm, d), F32)],
        compiler_params=_params(("parallel",)),
        name="out_ffn",
    )(x, hm, hf, ha, *consts)


FF_CHUNK = 256


def _prep_layer(l, w_in, b_mi, b_mf, b_ff, g_cq, w_uq, g_ckv, w_uk, w_uv, g_head, w_out,
                ln1_g, ln1_b, w_gate, w_up, w_down, ln2_g, ln2_b):
    d = w_in.shape[1]
    wm = H_MLSTM * HEAD_DIM
    wf = H_FOX * HEAD_DIM
    q_lora = g_cq.shape[1]
    split = (wm,) * 4 + (H_MLSTM, H_MLSTM) + (wf,) * 3 + (H_FOX,) + (q_lora, KV_LORA, ROPE_DIM)
    at = [int(i) for i in np.cumsum(split)[:-1]]
    mq, mk, mv, mo, mi, mf, fq, fk, fv, ff, cq, ckv, kr = jnp.split(w_in[l], at, axis=1)
    pad = jnp.zeros((d, LANES - ROPE_DIM - 2 * H_MLSTM - H_FOX), F32)
    w_perm = _bf(jnp.concatenate([mq, mk, mv, mo, fq, fk, fv, cq, ckv, kr, mi, mf, ff, pad], axis=1))
    gbias = jnp.zeros((1, LANES), F32)
    gbias = gbias.at[0, G_MI:G_MI + H_MLSTM].set(b_mi[l]).at[0, G_MF:G_MF + H_MLSTM].set(b_mf[l])
    gbias = gbias.at[0, G_FF:G_FF + H_FOX].set(b_ff[l])
    uq = w_uq[l].reshape(q_lora, H_MLA, NOPE_DIM + ROPE_DIM)
    eye = jnp.eye(H_MLA, dtype=F32)
    sel = np.zeros((H_MLA * ROPE_DIM, H_MLA * LANES), np.float32)
    for h in range(H_MLA):
        sel[h * ROPE_DIM + np.arange(ROPE_DIM), h * LANES + np.arange(ROPE_DIM)] = 1.0
    n_ff = w_gate.shape[2] // FF_CHUNK
    return dict(
        w_perm=w_perm, gbias=gbias,
        g_cq=g_cq[l][None], g_ckv=g_ckv[l][None],
        w_uq_n=_bf(uq[:, :, :NOPE_DIM].reshape(q_lora, H_MLA * NOPE_DIM)),
        w_uq_r=_bf(uq[:, :, NOPE_DIM:].reshape(q_lora, H_MLA * ROPE_DIM)),
        wuk_bd=_bf(jnp.einsum("chd,hg->hdgc", w_uk[l], eye).reshape(H_MLA * NOPE_DIM, H_MLA * KV_LORA)),
        sel=jnp.asarray(sel, BF16),
        wuv_pad=_bf(jnp.einsum("chd,hg->hcgd", w_uv[l], eye).reshape(H_MLA, KV_LORA, H_MLA * HEAD_DIM)),
        gh_m=g_head[l][None, :wm], gh_f=g_head[l][None, wm:wm + wf], gh_a=g_head[l][None, wm + wf:],
        wo_m=_bf(w_out[l][:wm]), wo_f=_bf(w_out[l][wm:wm + wf]), wo_a=_bf(w_out[l][wm + wf:]),
        ln1_g=ln1_g[l][None], ln1_b=ln1_b[l][None], ln2_g=ln2_g[l][None], ln2_b=ln2_b[l][None],
        wg=_bf(w_gate[l].reshape(d, n_ff, FF_CHUNK).transpose(1, 0, 2)),
        wu=_bf(w_up[l].reshape(d, n_ff, FF_CHUNK).transpose(1, 0, 2)),
        wd=_bf(w_down[l].reshape(n_ff, FF_CHUNK, d)),
    )


def _rope_tables(pos):
    half = ROPE_DIM // 2
    freq = ROPE_THETA ** (-jnp.arange(half, dtype=F32) * (2.0 / ROPE_DIM))
    ang = pos.astype(F32)[:, None] * freq[None, :]
    cos, sin = jnp.cos(ang), jnp.sin(ang)
    reps = 2 * LANES // ROPE_DIM
    return (jnp.tile(jnp.concatenate([cos, cos], axis=1), (1, reps)),
            jnp.tile(jnp.concatenate([-sin, sin], axis=1), (1, reps)))


def _block_diag_state(c):
    b = c.shape[0]
    eye = jnp.eye(H_MLSTM, dtype=c.dtype)
    return jnp.einsum("bhde,hg->bhdge", c, eye).reshape(b, H_MLSTM * HEAD_DIM, H_MLSTM * HEAD_DIM)


def _diag_blocks(cbd):
    b = cbd.shape[0]
    c5 = cbd.reshape(b, H_MLSTM, HEAD_DIM, H_MLSTM, HEAD_DIM)
    return jnp.stack([c5[:, h, :, h, :] for h in range(H_MLSTM)], axis=1)


def _layer_common(x3, lw, cs, sn, c0, n0, m0, n_valid, mla_bq, mla_tq, q_dtype):
    b, t, d = x3.shape
    zm, zf, zcq, zckv, zg, kb, vb = _inproj(x3.reshape(b * t, d), lw["w_perm"], lw["gbias"])
    zg3 = zg.reshape(b, t, LANES)
    gt3 = jnp.swapaxes(zg3[:, :, G_MI:G_MI + G_ROWS], 1, 2)
    hm, c1, n1, m1 = _mlstm(zm.reshape(b, t, -1), zg3, gt3, c0, n0, m0, lw["gh_m"], n_valid)
    qp, lat, latp = _mla_pre(zcq, zckv, zg, cs, sn, lw, mla_bq, mla_tq, q_dtype)
    return zf, (kb, vb), zg3, gt3, hm, (c1, n1, m1), qp, lat, latp


def _state_out(c1, n1, m1):
    b = c1.shape[0]
    return (_diag_blocks(c1), n1.reshape(b, H_MLSTM, HEAD_DIM), m1[:, :H_MLSTM, 0])


def kernel(x_prompt, x_sample, cache_fox_k, cache_fox_v, cache_fox_logf, cache_mla_latent, state_mlstm_C, state_mlstm_n, state_mlstm_m, page_table, w_in, b_mlstm_i, b_mlstm_f, b_fox_f, g_cq, w_uq, g_ckv, w_uk, w_uv, g_head, w_out, ln1_g, ln1_b, w_gate, w_up, w_down, ln2_g, ln2_b):
    depth = w_in.shape[0]
    bp, tp, d = x_prompt.shape
    bs, ts, _ = x_sample.shape
    n_pages = page_table.shape[1]
    page = cache_fox_k.shape[2]
    past = n_pages * page
    alpha = (2 * depth) ** 0.25
    wm = H_MLSTM * HEAD_DIM
    wf = H_FOX * HEAD_DIM

    pools = (jnp.transpose(cache_fox_k, (0, 1, 3, 4, 2)).reshape(depth, -1, wf, page),
             jnp.transpose(cache_fox_v, (0, 1, 3, 4, 2)).reshape(depth, -1, wf, page),
             jnp.transpose(cache_mla_latent, (0, 1, 3, 2)),
             jnp.transpose(cache_fox_logf, (0, 1, 3, 2)))

    cs_p, sn_p = _rope_tables(jnp.arange(tp, dtype=jnp.int32))
    cs_s, sn_s = _rope_tables(past + jnp.arange(PAD_T, dtype=jnp.int32))
    cs_s, sn_s = jnp.tile(cs_s, (bs, 1)), jnp.tile(sn_s, (bs, 1))

    yp = x_prompt
    ys = jnp.pad(x_sample, ((0, 0), (0, PAD_T - ts), (0, 0)))
    outs_p, outs_s = [], []
    for l in range(depth):
        lw = _prep_layer(l, w_in, b_mlstm_i, b_mlstm_f, b_fox_f, g_cq, w_uq, g_ckv, w_uk, w_uv, g_head, w_out,
                         ln1_g, ln1_b, w_gate, w_up, w_down, ln2_g, ln2_b)

        c0 = jnp.zeros((bp, wm, wm), F32)
        n0 = jnp.zeros((bp, 1, wm), F32)
        m0 = jnp.zeros((bp, SUBLANES, LANES), F32)
        zf, (kb, vb), zg3, gt3, hm, st, qp, lat, latp = _layer_common(yp, lw, cs_p, sn_p, c0, n0, m0, tp, bp, tp, BF16)
        fr3 = _fcum(gt3)
        zf3 = zf.reshape(bp, tp, -1)
        hf = _fox_prompt(zf3, kb.reshape(bp, tp, -1), vb.reshape(bp, tp, -1), fr3, lw["gh_f"])
        ha = _mla_prompt(qp, latp.reshape(bp, tp, -1), lw["wuv_pad"], lw["gh_a"])
        yp = _out_ffn(yp.reshape(bp * tp, d), hm.reshape(bp * tp, -1), hf.reshape(bp * tp, -1),
                      ha.reshape(bp * tp, -1), lw, alpha).reshape(bp, tp, d)
        outs_p.append((zf3[:, :, wf:2 * wf].reshape(bp, tp, H_FOX, HEAD_DIM),
                       zf3[:, :, 2 * wf:].reshape(bp, tp, H_FOX, HEAD_DIM),
                       zg3[:, :, G_FF:G_FF + H_FOX],
                       lat.reshape(bp, tp, -1)) + _state_out(*st))

        c0 = _block_diag_state(state_mlstm_C[l])
        n0 = state_mlstm_n[l].reshape(bs, 1, wm)
        m0 = jnp.broadcast_to(jnp.pad(state_mlstm_m[l], ((0, 0), (0, SUBLANES - H_MLSTM)))[:, :, None],
                              (bs, SUBLANES, LANES))
        zf, _, zg3, gt3, hm, st, qa, lat, latp = _layer_common(ys, lw, cs_s, sn_s, c0, n0, m0, ts, 1, bs * PAD_T, F32)
        zf3 = zf.reshape(bs, PAD_T, -1)
        valid = (jnp.arange(PAD_T) < ts)[None, :, None]

        def new_page(a):
            a = jnp.swapaxes(jnp.where(valid, a, 0.0), 1, 2)
            return jnp.pad(a, ((0, 0), (0, 0), (0, page - PAD_T)))

        new_pages = (new_page(zf3[:, :, wf:2 * wf]), new_page(zf3[:, :, 2 * wf:]),
                     new_page(lat.reshape(bs, PAD_T, -1)), new_page(zg3[:, :, G_FF:G_FF + H_FOX]))
        qa4 = jnp.swapaxes(qa.reshape(H_MLA, bs, PAD_T, 2 * LANES), 0, 1)
        hf, ha = _decode(page_table, l, pools, new_pages, zf3[:, :, :wf], qa4, lw["wuv_pad"], lw["gh_f"], lw["gh_a"])
        ys = _out_ffn(ys.reshape(bs * PAD_T, d), hm.reshape(bs * PAD_T, -1), hf.reshape(bs * PAD_T, -1),
                      ha.reshape(bs * PAD_T, -1), lw, alpha).reshape(bs, PAD_T, d)
        outs_s.append((zf3[:, :ts, wf:2 * wf].reshape(bs, ts, H_FOX, HEAD_DIM),
                       zf3[:, :ts, 2 * wf:].reshape(bs, ts, H_FOX, HEAD_DIM),
                       zg3[:, :ts, G_FF:G_FF + H_FOX],
                       lat.reshape(bs, PAD_T, -1)[:, :ts]) + _state_out(*st))

    sp = [jnp.stack(a) for a in zip(*outs_p)]
    ss = [jnp.stack(a) for a in zip(*outs_s)]
    return (yp, ys[:, :ts], sp[0], sp[1], sp[2], sp[3], sp[4], sp[5], sp[6],
            ss[0], ss[1], ss[2], ss[3], ss[4], ss[5], ss[6])
```

```python
import functools

import jax
import jax.numpy as jnp
import numpy as np
from jax import lax
from jax.experimental import pallas as pl
from jax.experimental.pallas import tpu as pltpu

F32 = jnp.float32
BF16 = jnp.bfloat16

HEAD_DIM = 64
H_MLSTM = 4
H_FOX = 4
H_MLA = 8
MLSTM_CHUNK = 128
KV_LORA = 128
NOPE_DIM = 64
ROPE_DIM = 32
ROPE_THETA = 10000.0
LN_EPS = 1e-5
RMS_EPS = 1e-6

LANES = 128
SUBLANES = 8
VMEM_LIMIT = 56 * 1024 * 1024

C_M = 0
C_F = 1024
C_CQ = 1792
C_CKV = 2048
C_G = 2176
D_INP = 2304
G_KR = 0
G_MI = 32
G_MF = 36
G_FF = 40
G_ROWS = 16

PAD_T = 8
NEG_INF = float("-inf")
LOG2E = 1.4426950408889634
STAGE_LAG = 2
ATT_TQ = 256
ATT_TK = 512
DECODE_PAGES = 32
MLSTM_SEQS = 4


def _bf(x):
    return x.astype(BF16)


def _dot(a, b):
    return jnp.dot(a, b, preferred_element_type=F32)


def _dot_nt(a, b):
    return lax.dot_general(a, b, (((1,), (1,)), ((), ())), preferred_element_type=F32)


def _dot_tn(a, b):
    return lax.dot_general(a, b, (((0,), (0,)), ((), ())), preferred_element_type=F32)


def _split3(x):
    hi = _bf(x)
    r = x - hi.astype(F32)
    mid = _bf(r)
    lo = _bf(r - mid.astype(F32))
    return hi, mid, lo


def _cumsum_rows(tril_bf, x):
    hi, mid, lo = _split3(x)
    return _dot(tril_bf, hi) + _dot(tril_bf, mid) + _dot(tril_bf, lo)


def _cumsum_lanes(x, triu_bf):
    hi, mid, lo = _split3(x)
    return _dot(hi, triu_bf) + _dot(mid, triu_bf) + _dot(lo, triu_bf)


def _tri(n, lower):
    r = lax.broadcasted_iota(jnp.int32, (n, n), 0)
    c = lax.broadcasted_iota(jnp.int32, (n, n), 1)
    return jnp.where((r >= c) if lower else (r <= c), 1.0, 0.0).astype(BF16)


def _log_sigmoid(x):
    return jnp.minimum(x, 0.0) - jnp.log1p(jnp.exp(-jnp.abs(x)))


def _head_mask(width, h):
    lane = lax.broadcasted_iota(jnp.int32, (1, width), 1)
    return (lane // HEAD_DIM) == h


def _head_rmsnorm(x, gh, n_heads):
    width = n_heads * HEAD_DIM
    x2 = x * x
    inv = jnp.zeros_like(x)
    for h in range(n_heads):
        mk = _head_mask(width, h)
        ms = jnp.sum(jnp.where(mk, x2, 0.0), axis=1, keepdims=True) * (1.0 / HEAD_DIM)
        inv = jnp.where(mk, lax.rsqrt(ms + RMS_EPS), inv)
    return x * inv * gh


def _layernorm(x, g, b):
    mu = jnp.mean(x, axis=1, keepdims=True)
    xc = x - mu
    var = jnp.mean(xc * xc, axis=1, keepdims=True)
    return xc * lax.rsqrt(var + LN_EPS) * g + b


def _params(sem):
    return pltpu.CompilerParams(dimension_semantics=sem, vmem_limit_bytes=VMEM_LIMIT)


def _token_tile(n):
    for t in (512, 256, 128, 64, 32, 16, 8):
        if n % t == 0:
            return t
    raise ValueError(f"token count {n} is not a multiple of 8")


def _const_spec(shape):
    nd = len(shape)
    return pl.BlockSpec(shape, lambda *_: (0,) * nd)


def _inproj_kernel(x_ref, w_ref, gb_ref, zm_ref, zf_ref, zcq_ref, zckv_ref, zg_ref, kb_ref, vb_ref):
    xb = _bf(x_ref[...])
    for c in range(4):
        zm_ref[:, c * 256:(c + 1) * 256] = _dot(xb, w_ref[:, C_M + c * 256:C_M + (c + 1) * 256])
    for c in range(3):
        z = _dot(xb, w_ref[:, C_F + c * 256:C_F + (c + 1) * 256])
        zf_ref[:, c * 256:(c + 1) * 256] = z
        if c == 1:
            kb_ref[...] = _bf(z)
        if c == 2:
            vb_ref[...] = _bf(z)
    zcq_ref[...] = _dot(xb, w_ref[:, C_CQ:C_CQ + 256])
    last = _dot(xb, w_ref[:, C_CKV:D_INP])
    zckv_ref[...] = last[:, :KV_LORA]
    g = last[:, KV_LORA:] + gb_ref[...]
    lane = lax.broadcasted_iota(jnp.int32, g.shape, 1)
    zg_ref[...] = jnp.where((lane >= G_MF) & (lane < G_FF + H_FOX), _log_sigmoid(g), g)


def _inproj(x, w_perm, gbias):
    n, d = x.shape
    tm = _token_tile(n)
    row = lambda w: pl.BlockSpec((tm, w), lambda i: (i, 0))
    wf = H_FOX * HEAD_DIM
    widths = (1024, 768, 256, KV_LORA, LANES, wf, wf)
    dtypes = (F32,) * 5 + (BF16,) * 2
    return pl.pallas_call(
        _inproj_kernel,
        grid=(n // tm,),
        in_specs=[row(d), _const_spec(w_perm.shape), _const_spec(gbias.shape)],
        out_specs=[row(w) for w in widths],
        out_shape=[jax.ShapeDtypeStruct((n, w), dt) for w, dt in zip(widths, dtypes)],
        compiler_params=_params(("parallel",)),
        name="inproj",
    )(x, w_perm, gbias)


def _mlstm_kernel(q_ref, k_ref, v_ref, o_ref, zg_ref, gt_ref, c0_ref, n0_ref, m0_ref, gh_ref,
                  hm_ref, c1_ref, n1_ref, m1_ref, c_sc, n_sc, m_sc, *, chunk, n_valid, n_seq):
    ci = pl.program_id(1)

    @pl.when(ci == 0)
    def _():
        c_sc[...] = c0_ref[...]
        n_sc[...] = n0_ref[...]
        m_sc[...] = m0_ref[...]

    for s in range(n_seq):
        _mlstm_chunk(q_ref.at[s], k_ref.at[s], v_ref.at[s], o_ref.at[s], zg_ref.at[s], gt_ref.at[s], gh_ref,
                     hm_ref.at[s], c_sc.at[s], n_sc.at[s], m_sc.at[s], chunk, n_valid)

    @pl.when(ci == pl.num_programs(1) - 1)
    def _():
        c1_ref[...] = c_sc[...]
        n1_ref[...] = n_sc[...]
        m1_ref[...] = m_sc[...]


def _mlstm_chunk(q_ref, k_ref, v_ref, o_ref, zg_ref, gt_ref, gh_ref, hm_ref, c_sc, n_sc, m_sc, L, n_valid):
    W = H_MLSTM * HEAD_DIM
    q = q_ref[...]
    k = k_ref[...] * (HEAD_DIM ** -0.5)
    v = v_ref[...]
    zg = zg_ref[...]
    gt = gt_ref[...]
    valid_c = lax.broadcasted_iota(jnp.int32, (L, 1), 0) < n_valid
    valid_r = lax.broadcasted_iota(jnp.int32, (1, L), 1) < n_valid
    cum_c = _cumsum_rows(_tri(L, True), jnp.where(valid_c, zg, 0.0))
    cum_r = _cumsum_lanes(jnp.where(valid_r, gt, 0.0), _tri(L, False))
    row = lax.broadcasted_iota(jnp.int32, (L, L), 0)
    col = lax.broadcasted_iota(jnp.int32, (L, L), 1)
    tril = row >= col
    qb, kb, vb = _bf(q), _bf(k), _bf(v)
    cb = _bf(c_sc[...])
    n_prev = n_sc[...]

    h_all = jnp.zeros((L, W), F32)
    w_all = jnp.zeros((L, W), F32)
    cdec_all = jnp.zeros((1, W), F32)
    for h in range(H_MLSTM):
        mk = _head_mask(W, h)
        b_c = cum_c[:, G_MF + h:G_MF + h + 1]
        b_r = cum_r[H_MLSTM + h:H_MLSTM + h + 1, :]
        ig_c = jnp.where(valid_c, zg[:, G_MI + h:G_MI + h + 1], NEG_INF)
        ig_r = jnp.where(valid_r, gt[h:h + 1, :], NEG_INF)
        m_prev = m_sc[h:h + 1, 0:1]
        dlog = jnp.where(tril, b_c - b_r + ig_r, NEG_INF)
        inter = b_c + m_prev
        mt = jnp.maximum(inter, jnp.max(dlog, axis=1, keepdims=True))
        dmat = jnp.exp(dlog - mt)
        qh = jnp.where(mk, q, 0.0)
        qhb = _bf(qh)
        s = _dot_nt(qhb, kb) * dmat
        dec = jnp.exp(inter - mt)
        num = _dot(_bf(s), vb) + dec * _dot(qhb, cb)
        den = jnp.sum(s, axis=1, keepdims=True) + dec * jnp.sum(qh * n_prev, axis=1, keepdims=True)
        hh = num / jnp.maximum(jnp.abs(den), jnp.exp(-mt))
        h_all = jnp.where(mk, hh, h_all)
        m_new = mt[L - 1:L, :]
        b_last = b_c[L - 1:L, :]
        w_c = jnp.exp(b_last - b_c + ig_c - m_new)
        cdec = jnp.exp(b_last + m_prev - m_new)
        w_all = jnp.where(mk, w_c, w_all)
        cdec_all = jnp.where(mk, cdec, cdec_all)
        m_sc[h:h + 1, :] = jnp.broadcast_to(m_new, (1, LANES))

    kw = k * w_all
    r2 = lax.broadcasted_iota(jnp.int32, (W, W), 0) // HEAD_DIM
    c2 = lax.broadcasted_iota(jnp.int32, (W, W), 1) // HEAD_DIM
    c_sc[...] = jnp.where(r2 == c2, cdec_all * c_sc[...] + _dot_tn(_bf(kw), vb), 0.0)
    n_sc[...] = cdec_all * n_prev + jnp.sum(kw, axis=0, keepdims=True)

    hm_ref[...] = _head_rmsnorm(h_all, gh_ref[...], H_MLSTM) * jax.nn.sigmoid(o_ref[...])


def _mlstm(zm3, zg3, gt3, c0, n0, m0, gh, n_valid):
    b, t, _ = zm3.shape
    chunk = MLSTM_CHUNK if t % MLSTM_CHUNK == 0 else t
    nc = t // chunk
    w = H_MLSTM * HEAD_DIM
    ns = MLSTM_SEQS if nc == 1 else 1
    while b % ns:
        ns //= 2
    col = lambda j: pl.BlockSpec((ns, chunk, w), lambda bi, ci: (bi, ci, j))
    state = lambda shape: pl.BlockSpec((ns,) + shape, lambda bi, ci: (bi, 0, 0))
    return pl.pallas_call(
        functools.partial(_mlstm_kernel, chunk=chunk, n_valid=n_valid, n_seq=ns),
        grid=(b // ns, nc),
        in_specs=[col(0), col(1), col(2), col(3),
                  pl.BlockSpec((ns, chunk, LANES), lambda bi, ci: (bi, ci, 0)),
                  pl.BlockSpec((ns, G_ROWS, chunk), lambda bi, ci: (bi, 0, ci)),
                  state((w, w)), state((1, w)), state((SUBLANES, LANES)),
                  _const_spec((1, w))],
        out_specs=[pl.BlockSpec((ns, chunk, w), lambda bi, ci: (bi, ci, 0)),
                   state((w, w)), state((1, w)), state((SUBLANES, LANES))],
        out_shape=[jax.ShapeDtypeStruct((b, t, w), F32),
                   jax.ShapeDtypeStruct((b, w, w), F32),
                   jax.ShapeDtypeStruct((b, 1, w), F32),
                   jax.ShapeDtypeStruct((b, SUBLANES, LANES), F32)],
        scratch_shapes=[pltpu.VMEM((ns, w, w), F32), pltpu.VMEM((ns, 1, w), F32),
                        pltpu.VMEM((ns, SUBLANES, LANES), F32)],
        compiler_params=_params(("parallel", "arbitrary")),
        name="mlstm",
    )(zm3, zm3, zm3, zm3, zg3, gt3, c0, n0, m0, gh)


def _fcum_kernel(gt_ref, fr_ref, cr_sc, *, chunk):
    L = chunk

    @pl.when(pl.program_id(1) == 0)
    def _():
        cr_sc[...] = jnp.zeros_like(cr_sc)

    cum_r = _cumsum_lanes(gt_ref[0], _tri(L, False)) + cr_sc[:, 0:1]
    fr_ref[0] = cum_r
    cr_sc[...] = jnp.broadcast_to(cum_r[:, L - 1:L], cr_sc.shape)


def _fcum(gt3):
    b, _, t = gt3.shape
    chunk = 512 if t % 512 == 0 else t
    return pl.pallas_call(
        functools.partial(_fcum_kernel, chunk=chunk),
        grid=(b, t // chunk),
        in_specs=[pl.BlockSpec((1, G_ROWS, chunk), lambda bi, ci: (bi, 0, ci))],
        out_specs=pl.BlockSpec((1, G_ROWS, chunk), lambda bi, ci: (bi, 0, ci)),
        out_shape=jax.ShapeDtypeStruct(gt3.shape, F32),
        scratch_shapes=[pltpu.VMEM((G_ROWS, LANES), F32)],
        compiler_params=_params(("parallel", "arbitrary")),
        name="fcum",
    )(gt3)


def _lane_tile(x, n):
    return x if n == 1 else jnp.concatenate([x] * n, axis=1)


def _staged(n, lag, *stages):
    for i in range(n + lag * (len(stages) - 1)):
        for s, stage in enumerate(stages):
            if 0 <= i - s * lag < n:
                stage(i - s * lag)


def _fox_kernel(q_ref, k_ref, v_ref, fr_ref, gh_ref, o_ref, qh_sc, acc_sc, m_sc, l_sc, *stage_sc, tq, tk):
    s_sc, p_sc, al_sc = stage_sc[0:H_FOX], stage_sc[H_FOX:2 * H_FOX], stage_sc[2 * H_FOX:]
    W = H_FOX * HEAD_DIM
    nt = tk // LANES
    qi = pl.program_id(1)
    n_full = (qi * tq) // tk
    q = q_ref[0]
    for h in range(H_FOX):
        qh_sc[h] = _bf(jnp.where(_head_mask(W, h), q, 0.0))
    acc_sc[...] = jnp.zeros_like(acc_sc)
    m_sc[...] = jnp.full_like(m_sc, NEG_INF)
    l_sc[...] = jnp.zeros_like(l_sc)

    def step(j, masked):
        off = pl.multiple_of(j * tk, tk)
        kb = k_ref[0, pl.ds(off, tk), :]
        vb = v_ref[0, pl.ds(off, tk), :]
        fk = fr_ref[0, :, pl.ds(off, tk)] * LOG2E
        if masked:
            qpos = qi * tq + lax.broadcasted_iota(jnp.int32, (tq, tk), 0)
            kpos = off + lax.broadcasted_iota(jnp.int32, (tq, tk), 1)
            causal = kpos <= qpos

        def scores(h):
            s_sc[h][...] = _dot_nt(qh_sc[h], kb)

        def softmax(h):
            t = s_sc[h][...] * (HEAD_DIM ** -0.5 * LOG2E) - fk[2 * H_MLSTM + h:2 * H_MLSTM + h + 1, :]
            if masked:
                t = jnp.where(causal, t, NEG_INF)
            m_prev = m_sc[h]
            m_new = jnp.maximum(m_prev, jnp.max(t, axis=1, keepdims=True))
            alpha = jnp.exp2(m_prev - m_new)
            p = jnp.exp2(t - _lane_tile(m_new, nt))
            l_sc[h] = alpha * l_sc[h] + jnp.sum(p, axis=1, keepdims=True)
            m_sc[h] = m_new
            al_sc[h][...] = alpha
            p_sc[h][...] = _bf(p)

        def values(h):
            acc_sc[h] = _lane_tile(al_sc[h][...], W // LANES) * acc_sc[h] + _dot(p_sc[h][...], vb)

        _staged(H_FOX, STAGE_LAG, scores, softmax, values)

    def full_step(j, carry):
        step(j, False)
        return carry

    lax.fori_loop(0, n_full, full_step, 0)
    step(n_full, True)

    out = jnp.zeros((tq, W), F32)
    for h in range(H_FOX):
        out = jnp.where(_head_mask(W, h), acc_sc[h] * _lane_tile(1.0 / l_sc[h], W // LANES), out)
    o_ref[0] = _head_rmsnorm(out, gh_ref[...], H_FOX)


def _fox_prompt(zf3, kb3, vb3, fr3, gh):
    b, t, _ = zf3.shape
    w = H_FOX * HEAD_DIM
    tq = ATT_TQ if t % ATT_TQ == 0 else t
    tk = ATT_TK if t % ATT_TK == 0 else t
    seq = lambda shape: pl.BlockSpec((1,) + shape, lambda bi, qi: (bi, 0, 0))
    return pl.pallas_call(
        functools.partial(_fox_kernel, tq=tq, tk=tk),
        grid=(b, t // tq),
        in_specs=[pl.BlockSpec((1, tq, w), lambda bi, qi: (bi, qi, 0)),
                  seq((t, w)), seq((t, w)), seq((G_ROWS, t)), _const_spec((1, w))],
        out_specs=pl.BlockSpec((1, tq, w), lambda bi, qi: (bi, qi, 0)),
        out_shape=jax.ShapeDtypeStruct((b, t, w), F32),
        scratch_shapes=([pltpu.VMEM((H_FOX, tq, w), BF16), pltpu.VMEM((H_FOX, tq, w), F32),
                         pltpu.VMEM((H_FOX, tq, LANES), F32), pltpu.VMEM((H_FOX, tq, LANES), F32)]
                        + [pltpu.VMEM((tq, tk), F32)] * H_FOX + [pltpu.VMEM((tq, tk), BF16)] * H_FOX
                        + [pltpu.VMEM((tq, LANES), F32)] * H_FOX),
        compiler_params=_params(("parallel", "arbitrary")),
        name="fox_prompt",
    )(zf3, kb3, vb3, fr3, gh)


def _rope_swap(x, group):
    w = x.shape[1]
    half = ROPE_DIM // 2
    lane = lax.broadcasted_iota(jnp.int32, x.shape, 1)
    first = (lane % group) < half
    return jnp.where(first, pltpu.roll(x, w - half, 1), pltpu.roll(x, half, 1))


def _mla_pre_kernel(zcq_ref, zckv_ref, zg_ref, cs_ref, sn_ref, gcq_ref, wn_ref, wr_ref, gckv_ref, wuk_ref, sel_ref,
                    q_ref, lat_ref, latp_ref):
    zcq = zcq_ref[...]
    cqn = zcq * lax.rsqrt(jnp.mean(zcq * zcq, axis=1, keepdims=True) + RMS_EPS) * gcq_ref[...]
    cqb = _bf(cqn)
    qn = _dot(cqb, wn_ref[...])
    qr = _dot(cqb, wr_ref[...])
    cs = cs_ref[...]
    sn = sn_ref[...]
    qrot = qr * cs + _rope_swap(qr, ROPE_DIM) * sn
    qlat = _dot(_bf(qn), wuk_ref[...])
    qrope = _dot(_bf(qrot), sel_ref[...])
    for h in range(H_MLA):
        q_ref[0, h, :, 0:LANES] = qlat[:, h * LANES:(h + 1) * LANES].astype(q_ref.dtype)
        q_ref[0, h, :, LANES:2 * LANES] = qrope[:, h * LANES:(h + 1) * LANES].astype(q_ref.dtype)

    zckv = zckv_ref[...]
    ckvn = zckv * lax.rsqrt(jnp.mean(zckv * zckv, axis=1, keepdims=True) + RMS_EPS) * gckv_ref[...]
    zg = zg_ref[...]
    krot = zg * cs[:, :LANES] + _rope_swap(zg, ROPE_DIM) * sn[:, :LANES]
    lane = lax.broadcasted_iota(jnp.int32, krot.shape, 1)
    krz = jnp.where(lane < ROPE_DIM, krot, jnp.where(lane == LANES - 1, 1.0, 0.0))
    lat_ref[:, 0:KV_LORA] = ckvn
    lat_ref[:, KV_LORA:KV_LORA + ROPE_DIM] = krot[:, 0:ROPE_DIM]
    latp_ref[:, 0:KV_LORA] = _bf(ckvn)
    latp_ref[:, KV_LORA:2 * KV_LORA] = _bf(krz)


def _mla_pre(zcq, zckv, zg, cs, sn, lw, bq, tq, q_dtype):
    n = zcq.shape[0]
    tm = _token_tile(min(n, tq))
    tm = min(tm, tq)
    nt_tab = cs.shape[0] // tm
    nt_q = tq // tm
    row = lambda w: pl.BlockSpec((tm, w), lambda i: (i, 0))
    tab = pl.BlockSpec((tm, 2 * LANES), lambda i: (i % nt_tab, 0))
    consts = (lw["g_cq"], lw["w_uq_n"], lw["w_uq_r"], lw["g_ckv"], lw["wuk_bd"], lw["sel"])
    return pl.pallas_call(
        _mla_pre_kernel,
        grid=(n // tm,),
        in_specs=[row(256), row(KV_LORA), row(LANES), tab, tab] + [_const_spec(c.shape) for c in consts],
        out_specs=[pl.BlockSpec((1, H_MLA, tm, 2 * LANES), lambda i: (i // nt_q, 0, i % nt_q, 0)),
                   row(KV_LORA + ROPE_DIM), row(2 * LANES)],
        out_shape=[jax.ShapeDtypeStruct((bq, H_MLA, tq, 2 * LANES), q_dtype),
                   jax.ShapeDtypeStruct((n, KV_LORA + ROPE_DIM), F32),
                   jax.ShapeDtypeStruct((n, 2 * LANES), BF16)],
        compiler_params=_params(("parallel",)),
        name="mla_pre",
    )(zcq, zckv, zg, cs, sn, *consts)


def _mla_out(acc, inv_l, wuv_ref, gh, rows):
    out = jnp.zeros((rows, H_MLA * HEAD_DIM), F32)
    for h in range(H_MLA):
        o_h = acc[h * rows:(h + 1) * rows] * inv_l[h * rows:(h + 1) * rows]
        out = out + _dot(_bf(o_h), wuv_ref[h])
    return _head_rmsnorm(out, gh, H_MLA)


def _mla_kernel(q_ref, k_ref, wuv_ref, gh_ref, o_ref, acc_sc, m_sc, *stage_sc, tq, tk):
    s_sc, p_sc, al_sc = stage_sc[0:H_MLA], stage_sc[H_MLA:2 * H_MLA], stage_sc[2 * H_MLA:]
    nt = tk // LANES
    scale2 = (NOPE_DIM + ROPE_DIM) ** -0.5 * LOG2E
    qi = pl.program_id(1)
    n_full = (qi * tq) // tk
    acc_sc[...] = jnp.zeros_like(acc_sc)
    m_sc[...] = jnp.full_like(m_sc, NEG_INF)

    def step(j, masked):
        off = pl.multiple_of(j * tk, tk)
        kb = k_ref[0, pl.ds(off, tk), :]
        if masked:
            qpos = qi * tq + lax.broadcasted_iota(jnp.int32, (tq, tk), 0)
            kpos = off + lax.broadcasted_iota(jnp.int32, (tq, tk), 1)
            causal = kpos <= qpos
        def scores(h):
            s_sc[h][...] = _dot_nt(q_ref[0, h], kb)

        def softmax(h):
            s = s_sc[h][...]
            if masked:
                s = jnp.where(causal, s, NEG_INF)
            m_prev = m_sc[h]
            m_new = jnp.maximum(m_prev, jnp.max(s, axis=1, keepdims=True))
            al_sc[h][...] = jnp.exp2((m_prev - m_new) * scale2)
            p_sc[h][...] = _bf(jnp.exp2((s - _lane_tile(m_new, nt)) * scale2))
            m_sc[h] = m_new

        def values(h):
            acc_sc[h] = _lane_tile(al_sc[h][...], 2) * acc_sc[h] + _dot(p_sc[h][...], kb)

        _staged(H_MLA, STAGE_LAG, scores, softmax, values)

    def full_step(j, carry):
        step(j, False)
        return carry

    lax.fori_loop(0, n_full, full_step, 0)
    step(n_full, True)

    out = jnp.zeros((tq, H_MLA * HEAD_DIM), F32)
    for h in range(H_MLA):
        acc = acc_sc[h]
        o_h = acc[:, :KV_LORA] * (1.0 / acc[:, 2 * LANES - 1:2 * LANES])
        out = out + _dot(_bf(o_h), wuv_ref[h])
    o_ref[0] = _head_rmsnorm(out, gh_ref[...], H_MLA)


def _mla_prompt(qp, latp3, wuv_pad, gh):
    b, _, t, _ = qp.shape
    tq = ATT_TQ if t % ATT_TQ == 0 else t
    tk = ATT_TK if t % ATT_TK == 0 else t
    w = H_MLA * HEAD_DIM
    return pl.pallas_call(
        functools.partial(_mla_kernel, tq=tq, tk=tk),
        grid=(b, t // tq),
        in_specs=[pl.BlockSpec((1, H_MLA, tq, 2 * LANES), lambda bi, qi: (bi, 0, qi, 0)),
                  pl.BlockSpec((1, t, 2 * LANES), lambda bi, qi: (bi, 0, 0)),
                  _const_spec(wuv_pad.shape), _const_spec((1, w))],
        out_specs=pl.BlockSpec((1, tq, w), lambda bi, qi: (bi, qi, 0)),
        out_shape=jax.ShapeDtypeStruct((b, t, w), F32),
        scratch_shapes=([pltpu.VMEM((H_MLA, tq, 2 * LANES), F32), pltpu.VMEM((H_MLA, tq, LANES), F32)]
                        + [pltpu.VMEM((tq, tk), F32)] * H_MLA + [pltpu.VMEM((tq, tk), BF16)] * H_MLA
                        + [pltpu.VMEM((tq, LANES), F32)] * H_MLA),
        compiler_params=_params(("parallel", "arbitrary")),
        name="mla_prompt",
    )(qp, latp3, wuv_pad, gh)


def _decode_kernel(pt_ref, kt_hbm, vt_hbm, lt_hbm, lf_hbm, ktn_ref, vtn_ref, ltn_ref, lfn_ref, qf_ref, qa_ref,
                   wuv_ref, ghf_ref, gha_ref, of_ref, oa_ref,
                   k_buf, v_buf, l_buf, f_buf, sems, lat16, lfn_buf, facc, fm, fl, aacc, am, al, carry,
                   *, n_group, layer):
    G = n_group
    bi = pl.program_id(0)
    j = pl.program_id(1)
    nj = pl.num_programs(1)
    step = bi * nj + j
    last_step = pl.num_programs(0) * nj - 1
    WF = H_FOX * HEAD_DIM
    RF = H_FOX * PAD_T
    RA = H_MLA * PAD_T

    def page_copies(s, slot):
        b_s = s // nj
        first = (s % nj) * 2 * G + slot * G
        out = []
        for g in range(G):
            page = pt_ref[b_s, first + g]
            out.append(pltpu.make_async_copy(kt_hbm.at[layer, page], k_buf.at[slot, g], sems.at[slot, 0]))
            out.append(pltpu.make_async_copy(vt_hbm.at[layer, page], v_buf.at[slot, g], sems.at[slot, 1]))
            out.append(pltpu.make_async_copy(lt_hbm.at[layer, page], l_buf.at[slot, g], sems.at[slot, 2]))
            out.append(pltpu.make_async_copy(lf_hbm.at[layer, page],
                                             f_buf.at[slot, pl.ds(g * SUBLANES, H_FOX), :], sems.at[slot, 3]))
        return out

    @pl.when(step == 0)
    def _():
        for slot in range(2):
            for g in range(G):
                f_buf[slot, g * SUBLANES + H_FOX:(g + 1) * SUBLANES, :] = jnp.zeros((SUBLANES - H_FOX, LANES), F32)
        lat16[...] = jnp.zeros_like(lat16)
        lfn_buf[...] = jnp.zeros_like(lfn_buf)
        for c in page_copies(step, 0):
            c.start()

    @pl.when(j == 0)
    def _():
        facc[...] = jnp.zeros_like(facc)
        fm[...] = jnp.full_like(fm, NEG_INF)
        fl[...] = jnp.zeros_like(fl)
        aacc[...] = jnp.zeros_like(aacc)
        am[...] = jnp.full_like(am, NEG_INF)
        al[...] = jnp.zeros_like(al)
        carry[...] = jnp.zeros_like(carry)

    q = qf_ref[0]
    qbd = jnp.concatenate([_bf(jnp.where(_head_mask(WF, h), q, 0.0)) for h in range(H_FOX)], axis=0)
    qa = _bf(qa_ref[0].reshape(RA, 2 * LANES))
    triu = _tri(LANES, False)
    scale_f = HEAD_DIM ** -0.5 * LOG2E
    scale_a = (NOPE_DIM + ROPE_DIM) ** -0.5 * LOG2E

    def block(kts, vts, lts, lf8, kbuf, new_tokens):
        n = len(kts)
        cs = _cumsum_lanes(lf8, triu)
        tot = jnp.broadcast_to(cs[:, LANES - 1:LANES], cs.shape)
        base = jnp.concatenate([carry[...]] * n, axis=0)
        if n > 1:
            r = lax.broadcasted_iota(jnp.int32, (n * SUBLANES, n * SUBLANES), 0)
            c = lax.broadcasted_iota(jnp.int32, (n * SUBLANES, n * SUBLANES), 1)
            earlier = jnp.where((r % SUBLANES == c % SUBLANES) & (c // SUBLANES < r // SUBLANES), 1.0, 0.0)
            base = base + _cumsum_rows(earlier.astype(BF16), tot)
        if not new_tokens:
            last = slice((n - 1) * SUBLANES, n * SUBLANES)
            carry[...] = base[last] + tot[last]
        cum = (cs + base) * LOG2E
        s_f, s_a = [], []
        for g in range(n):
            bias = jnp.concatenate([jnp.broadcast_to(cum[g * SUBLANES + h:g * SUBLANES + h + 1, :], (PAD_T, LANES))
                                    for h in range(H_FOX)], axis=0)
            s_f.append(_dot(qbd, _bf(kts[g][...])) * scale_f - bias)
            kbuf[g, 0:KV_LORA + ROPE_DIM, :] = _bf(lts[g][...])
            s_a.append(_dot(qa, kbuf[g]))
        s_f = jnp.concatenate(s_f, axis=1)
        s_a = jnp.concatenate(s_a, axis=1)
        if new_tokens:
            ok_f = lax.broadcasted_iota(jnp.int32, s_f.shape, 1) <= lax.broadcasted_iota(jnp.int32, s_f.shape, 0) % PAD_T
            ok_a = lax.broadcasted_iota(jnp.int32, s_a.shape, 1) <= lax.broadcasted_iota(jnp.int32, s_a.shape, 0) % PAD_T
            s_f = jnp.where(ok_f, s_f, NEG_INF)
            s_a = jnp.where(ok_a, s_a, NEG_INF)
        m_prev = fm[...]
        m_new = jnp.maximum(m_prev, jnp.max(s_f, axis=1, keepdims=True))
        alpha = jnp.exp2(m_prev - m_new)
        p = jnp.exp2(s_f - _lane_tile(m_new, n))
        fl[...] = alpha * fl[...] + jnp.sum(p, axis=1, keepdims=True)
        fm[...] = m_new
        pv = jnp.zeros((RF, WF), F32)
        for g in range(n):
            pv = pv + _dot_nt(_bf(p[:, g * LANES:(g + 1) * LANES]), _bf(vts[g][...]))
        facc[...] = _lane_tile(alpha, WF // LANES) * facc[...] + pv
        m_prev = am[...]
        m_new = jnp.maximum(m_prev, jnp.max(s_a, axis=1, keepdims=True))
        alpha = jnp.exp2((m_prev - m_new) * scale_a)
        p = jnp.exp2((s_a - _lane_tile(m_new, n)) * scale_a)
        al[...] = alpha * al[...] + jnp.sum(p, axis=1, keepdims=True)
        am[...] = m_new
        pv = jnp.zeros((RA, KV_LORA), F32)
        for g in range(n):
            pv = pv + _dot_nt(_bf(p[:, g * LANES:(g + 1) * LANES]), kbuf[g, 0:KV_LORA, :])
        aacc[...] = alpha * aacc[...] + pv

    def half_step(slot):
        for c in page_copies(step, slot):
            c.wait()
        block([k_buf.at[slot, g] for g in range(G)], [v_buf.at[slot, g] for g in range(G)],
              [l_buf.at[slot, g] for g in range(G)], f_buf[slot], lat16.at[slot], False)

    for c in page_copies(step, 1):
        c.start()
    half_step(0)
    nxt = jnp.minimum(step + 1, last_step)
    for c in page_copies(nxt, 0):
        c.start()
    half_step(1)

    @pl.when(step == last_step)
    def _():
        for c in page_copies(step, 0):
            c.wait()

    @pl.when(j == nj - 1)
    def _():
        lfn_buf[0:H_FOX, :] = lfn_ref[0]
        block([ktn_ref.at[0]], [vtn_ref.at[0]], [ltn_ref.at[0]], lfn_buf[...], lat16.at[0], True)
        accf = facc[...] * _lane_tile(1.0 / fl[...], WF // LANES)
        of = jnp.zeros((PAD_T, WF), F32)
        for h in range(H_FOX):
            of = jnp.where(_head_mask(WF, h), accf[h * PAD_T:(h + 1) * PAD_T], of)
        of_ref[0] = _head_rmsnorm(of, ghf_ref[...], H_FOX)
        oa_ref[0] = _mla_out(aacc[...], 1.0 / al[...], wuv_ref, gha_ref[...], PAD_T)


def _decode(page_table, layer, caches, new_pages, qf, qa, wuv_pad, ghf, gha):
    b, n_pages = page_table.shape
    assert n_pages % 2 == 0, "the two-slot page pipeline needs an even number of pages per sequence"
    G = DECODE_PAGES
    while n_pages % (2 * G):
        G //= 2
    kt, vt, lt, lf = caches
    page = kt.shape[-1]

    def seq_spec(arr):
        nd = arr.ndim
        return pl.BlockSpec((1,) + arr.shape[1:], lambda bi, ji, pt: (bi,) + (0,) * (nd - 1))

    def const(arr):
        nd = arr.ndim
        return pl.BlockSpec(arr.shape, lambda bi, ji, pt: (0,) * nd)

    in_specs = [pl.BlockSpec(memory_space=pl.ANY)] * 4
    operands = [kt, vt, lt, lf]
    for arr in tuple(new_pages) + (qf, qa):
        in_specs.append(seq_spec(arr))
        operands.append(arr)
    for arr in (wuv_pad, ghf, gha):
        in_specs.append(const(arr))
        operands.append(arr)
    wf = H_FOX * HEAD_DIM
    wa = H_MLA * HEAD_DIM
    grid_spec = pltpu.PrefetchScalarGridSpec(
        num_scalar_prefetch=1,
        grid=(b, n_pages // (2 * G)),
        in_specs=in_specs,
        out_specs=[pl.BlockSpec((1, PAD_T, wf), lambda bi, ji, pt: (bi, 0, 0)),
                   pl.BlockSpec((1, PAD_T, wa), lambda bi, ji, pt: (bi, 0, 0))],
        scratch_shapes=[pltpu.VMEM((2, G) + kt.shape[2:], F32), pltpu.VMEM((2, G) + vt.shape[2:], F32),
                        pltpu.VMEM((2, G) + lt.shape[2:], F32), pltpu.VMEM((2, G * SUBLANES, page), F32),
                        pltpu.SemaphoreType.DMA((2, 4)),
                        pltpu.VMEM((2, G, 2 * LANES, page), BF16), pltpu.VMEM((SUBLANES, page), F32),
                        pltpu.VMEM((H_FOX * PAD_T, wf), F32), pltpu.VMEM((H_FOX * PAD_T, LANES), F32),
                        pltpu.VMEM((H_FOX * PAD_T, LANES), F32),
                        pltpu.VMEM((H_MLA * PAD_T, KV_LORA), F32), pltpu.VMEM((H_MLA * PAD_T, LANES), F32),
                        pltpu.VMEM((H_MLA * PAD_T, LANES), F32),
                        pltpu.VMEM((SUBLANES, LANES), F32)])
    return pl.pallas_call(
        functools.partial(_decode_kernel, n_group=G, layer=layer),
        grid_spec=grid_spec,
        out_shape=[jax.ShapeDtypeStruct((b, PAD_T, wf), F32), jax.ShapeDtypeStruct((b, PAD_T, wa), F32)],
        compiler_params=_params(("arbitrary", "arbitrary")),
        name="decode",
    )(page_table, *operands)


def _out_ffn_kernel(x_ref, hm_ref, hf_ref, ha_ref, wom_ref, wof_ref, woa_ref, g1_ref, b1_ref,
                    wg_ref, wu_ref, wd_ref, g2_ref, b2_ref, y_ref, acc_sc, *, alpha, n_chunks):
    mix = (_dot(_bf(hm_ref[...]), wom_ref[...]) + _dot(_bf(hf_ref[...]), wof_ref[...])
           + _dot(_bf(ha_ref[...]), woa_ref[...]))
    x1 = _layernorm(alpha * x_ref[...] + mix, g1_ref[...], b1_ref[...])
    x1b = _bf(x1)
    acc_sc[...] = jnp.zeros_like(acc_sc)

    def body(c, carry):
        g = _dot(x1b, wg_ref[c])
        u = _dot(x1b, wu_ref[c])
        acc_sc[...] += _dot(_bf(g * jax.nn.sigmoid(g) * u), wd_ref[c])
        return carry

    lax.fori_loop(0, n_chunks, body, 0)
    y_ref[...] = _layernorm(alpha * x1 + acc_sc[...], g2_ref[...], b2_ref[...])


def _out_ffn(x, hm, hf, ha, lw, alpha):
    n, d = x.shape
    tm = _token_tile(n)
    row = lambda w: pl.BlockSpec((tm, w), lambda i: (i, 0))
    consts = (lw["wo_m"], lw["wo_f"], lw["wo_a"], lw["ln1_g"], lw["ln1_b"],
              lw["wg"], lw["wu"], lw["wd"], lw["ln2_g"], lw["ln2_b"])

    def resident(c):
        nd = c.ndim
        return pl.BlockSpec(c.shape, lambda i: (0,) * nd, pipeline_mode=pl.Buffered(1))

    return pl.pallas_call(
        functools.partial(_out_ffn_kernel, alpha=alpha, n_chunks=lw["wg"].shape[0]),
        grid=(n // tm,),
        in_specs=[row(d), row(hm.shape[1]), row(hf.shape[1]), row(ha.shape[1])] + [resident(c) for c in consts],
        out_specs=row(d),
        out_shape=jax.ShapeDtypeStruct((n, d), F32),
        scratch_shapes=[pltpu.VMEM((tm, d), F32)],
        compiler_params=_params(("parallel",)),
        name="out_ffn",
    )(x, hm, hf, ha, *consts)


FF_CHUNK = 256


def _prep_layer(l, w_in, b_mi, b_mf, b_ff, g_cq, w_uq, g_ckv, w_uk, w_uv, g_head, w_out,
                ln1_g, ln1_b, w_gate, w_up, w_down, ln2_g, ln2_b):
    d = w_in.shape[1]
    wm = H_MLSTM * HEAD_DIM
    wf = H_FOX * HEAD_DIM
    q_lora = g_cq.shape[1]
    split = (wm,) * 4 + (H_MLSTM, H_MLSTM) + (wf,) * 3 + (H_FOX,) + (q_lora, KV_LORA, ROPE_DIM)
    at = [int(i) for i in np.cumsum(split)[:-1]]
    mq, mk, mv, mo, mi, mf, fq, fk, fv, ff, cq, ckv, kr = jnp.split(w_in[l], at, axis=1)
    pad = jnp.zeros((d, LANES - ROPE_DIM - 2 * H_MLSTM - H_FOX), F32)
    w_perm = _bf(jnp.concatenate([mq, mk, mv, mo, fq, fk, fv, cq, ckv, kr, mi, mf, ff, pad], axis=1))
    gbias = jnp.zeros((1, LANES), F32)
    gbias = gbias.at[0, G_MI:G_MI + H_MLSTM].set(b_mi[l]).at[0, G_MF:G_MF + H_MLSTM].set(b_mf[l])
    gbias = gbias.at[0, G_FF:G_FF + H_FOX].set(b_ff[l])
    uq = w_uq[l].reshape(q_lora, H_MLA, NOPE_DIM + ROPE_DIM)
    eye = jnp.eye(H_MLA, dtype=F32)
    sel = np.zeros((H_MLA * ROPE_DIM, H_MLA * LANES), np.float32)
    for h in range(H_MLA):
        sel[h * ROPE_DIM + np.arange(ROPE_DIM), h * LANES + np.arange(ROPE_DIM)] = 1.0
    n_ff = w_gate.shape[2] // FF_CHUNK
    return dict(
        w_perm=w_perm, gbias=gbias,
        g_cq=g_cq[l][None], g_ckv=g_ckv[l][None],
        w_uq_n=_bf(uq[:, :, :NOPE_DIM].reshape(q_lora, H_MLA * NOPE_DIM)),
        w_uq_r=_bf(uq[:, :, NOPE_DIM:].reshape(q_lora, H_MLA * ROPE_DIM)),
        wuk_bd=_bf(jnp.einsum("chd,hg->hdgc", w_uk[l], eye).reshape(H_MLA * NOPE_DIM, H_MLA * KV_LORA)),
        sel=jnp.asarray(sel, BF16),
        wuv_pad=_bf(jnp.einsum("chd,hg->hcgd", w_uv[l], eye).reshape(H_MLA, KV_LORA, H_MLA * HEAD_DIM)),
        gh_m=g_head[l][None, :wm], gh_f=g_head[l][None, wm:wm + wf], gh_a=g_head[l][None, wm + wf:],
        wo_m=_bf(w_out[l][:wm]), wo_f=_bf(w_out[l][wm:wm + wf]), wo_a=_bf(w_out[l][wm + wf:]),
        ln1_g=ln1_g[l][None], ln1_b=ln1_b[l][None], ln2_g=ln2_g[l][None], ln2_b=ln2_b[l][None],
        wg=_bf(w_gate[l].reshape(d, n_ff, FF_CHUNK).transpose(1, 0, 2)),
        wu=_bf(w_up[l].reshape(d, n_ff, FF_CHUNK).transpose(1, 0, 2)),
        wd=_bf(w_down[l].reshape(n_ff, FF_CHUNK, d)),
    )


def _rope_tables(pos):
    half = ROPE_DIM // 2
    freq = ROPE_THETA ** (-jnp.arange(half, dtype=F32) * (2.0 / ROPE_DIM))
    ang = pos.astype(F32)[:, None] * freq[None, :]
    cos, sin = jnp.cos(ang), jnp.sin(ang)
    reps = 2 * LANES // ROPE_DIM
    return (jnp.tile(jnp.concatenate([cos, cos], axis=1), (1, reps)),
            jnp.tile(jnp.concatenate([-sin, sin], axis=1), (1, reps)))


def _block_diag_state(c):
    b = c.shape[0]
    eye = jnp.eye(H_MLSTM, dtype=c.dtype)
    return jnp.einsum("bhde,hg->bhdge", c, eye).reshape(b, H_MLSTM * HEAD_DIM, H_MLSTM * HEAD_DIM)


def _diag_blocks(cbd):
    b = cbd.shape[0]
    c5 = cbd.reshape(b, H_MLSTM, HEAD_DIM, H_MLSTM, HEAD_DIM)
    return jnp.stack([c5[:, h, :, h, :] for h in range(H_MLSTM)], axis=1)


def _layer_common(x3, lw, cs, sn, c0, n0, m0, n_valid, mla_bq, mla_tq, q_dtype):
    b, t, d = x3.shape
    zm, zf, zcq, zckv, zg, kb, vb = _inproj(x3.reshape(b * t, d), lw["w_perm"], lw["gbias"])
    zg3 = zg.reshape(b, t, LANES)
    gt3 = jnp.swapaxes(zg3[:, :, G_MI:G_MI + G_ROWS], 1, 2)
    hm, c1, n1, m1 = _mlstm(zm.reshape(b, t, -1), zg3, gt3, c0, n0, m0, lw["gh_m"], n_valid)
    qp, lat, latp = _mla_pre(zcq, zckv, zg, cs, sn, lw, mla_bq, mla_tq, q_dtype)
    return zf, (kb, vb), zg3, gt3, hm, (c1, n1, m1), qp, lat, latp


def _state_out(c1, n1, m1):
    b = c1.shape[0]
    return (_diag_blocks(c1), n1.reshape(b, H_MLSTM, HEAD_DIM), m1[:, :H_MLSTM, 0])


def kernel(x_prompt, x_sample, cache_fox_k, cache_fox_v, cache_fox_logf, cache_mla_latent, state_mlstm_C, state_mlstm_n, state_mlstm_m, page_table, w_in, b_mlstm_i, b_mlstm_f, b_fox_f, g_cq, w_uq, g_ckv, w_uk, w_uv, g_head, w_out, ln1_g, ln1_b, w_gate, w_up, w_down, ln2_g, ln2_b):
    depth = w_in.shape[0]
    bp, tp, d = x_prompt.shape
    bs, ts, _ = x_sample.shape
    n_pages = page_table.shape[1]
    page = cache_fox_k.shape[2]
    past = n_pages * page
    alpha = (2 * depth) ** 0.25
    wm = H_MLSTM * HEAD_DIM
    wf = H_FOX * HEAD_DIM

    pools = (jnp.transpose(cache_fox_k, (0, 1, 3, 4, 2)).reshape(depth, -1, wf, page),
             jnp.transpose(cache_fox_v, (0, 1, 3, 4, 2)).reshape(depth, -1, wf, page),
             jnp.transpose(cache_mla_latent, (0, 1, 3, 2)),
             jnp.transpose(cache_fox_logf, (0, 1, 3, 2)))

    cs_p, sn_p = _rope_tables(jnp.arange(tp, dtype=jnp.int32))
    cs_s, sn_s = _rope_tables(past + jnp.arange(PAD_T, dtype=jnp.int32))
    cs_s, sn_s = jnp.tile(cs_s, (bs, 1)), jnp.tile(sn_s, (bs, 1))

    yp = x_prompt
    ys = jnp.pad(x_sample, ((0, 0), (0, PAD_T - ts), (0, 0)))
    outs_p, outs_s = [], []
    for l in range(depth):
        lw = _prep_layer(l, w_in, b_mlstm_i, b_mlstm_f, b_fox_f, g_cq, w_uq, g_ckv, w_uk, w_uv, g_head, w_out,
                         ln1_g, ln1_b, w_gate, w_up, w_down, ln2_g, ln2_b)

        c0 = jnp.zeros((bp, wm, wm), F32)
        n0 = jnp.zeros((bp, 1, wm), F32)
        m0 = jnp.zeros((bp, SUBLANES, LANES), F32)
        zf, (kb, vb), zg3, gt3, hm, st, qp, lat, latp = _layer_common(yp, lw, cs_p, sn_p, c0, n0, m0, tp, bp, tp, BF16)
        fr3 = _fcum(gt3)
        zf3 = zf.reshape(bp, tp, -1)
        hf = _fox_prompt(zf3, kb.reshape(bp, tp, -1), vb.reshape(bp, tp, -1), fr3, lw["gh_f"])
        ha = _mla_prompt(qp, latp.reshape(bp, tp, -1), lw["wuv_pad"], lw["gh_a"])
        yp = _out_ffn(yp.reshape(bp * tp, d), hm.reshape(bp * tp, -1), hf.reshape(bp * tp, -1),
                      ha.reshape(bp * tp, -1), lw, alpha).reshape(bp, tp, d)
        outs_p.append((zf3[:, :, wf:2 * wf].reshape(bp, tp, H_FOX, HEAD_DIM),
                       zf3[:, :, 2 * wf:].reshape(bp, tp, H_FOX, HEAD_DIM),
                       zg3[:, :, G_FF:G_FF + H_FOX],
                       lat.reshape(bp, tp, -1)) + _state_out(*st))

        c0 = _block_diag_state(state_mlstm_C[l])
        n0 = state_mlstm_n[l].reshape(bs, 1, wm)
        m0 = jnp.broadcast_to(jnp.pad(state_mlstm_m[l], ((0, 0), (0, SUBLANES - H_MLSTM)))[:, :, None],
                              (bs, SUBLANES, LANES))
        zf, _, zg3, gt3, hm, st, qa, lat, latp = _layer_common(ys, lw, cs_s, sn_s, c0, n0, m0, ts, 1, bs * PAD_T, F32)
        zf3 = zf.reshape(bs, PAD_T, -1)
        valid = (jnp.arange(PAD_T) < ts)[None, :, None]

        def new_page(a):
            a = jnp.swapaxes(jnp.where(valid, a, 0.0), 1, 2)
            return jnp.pad(a, ((0, 0), (0, 0), (0, page - PAD_T)))

        new_pages = (new_page(zf3[:, :, wf:2 * wf]), new_page(zf3[:, :, 2 * wf:]),
                     new_page(lat.reshape(bs, PAD_T, -1)), new_page(zg3[:, :, G_FF:G_FF + H_FOX]))
        qa4 = jnp.swapaxes(qa.reshape(H_MLA, bs, PAD_T, 2 * LANES), 0, 1)
        hf, ha = _decode(page_table, l, pools, new_pages, zf3[:, :, :wf], qa4, lw["wuv_pad"], lw["gh_f"], lw["gh_a"])
        ys = _out_ffn(ys.reshape(bs * PAD_T, d), hm.reshape(bs * PAD_T, -1), hf.reshape(bs * PAD_T, -1),
                      ha.reshape(bs * PAD_T, -1), lw, alpha).reshape(bs, PAD_T, d)
        outs_s.append((zf3[:, :ts, wf:2 * wf].reshape(bs, ts, H_FOX, HEAD_DIM),
                       zf3[:, :ts, 2 * wf:].reshape(bs, ts, H_FOX, HEAD_DIM),
                       zg3[:, :ts, G_FF:G_FF + H_FOX],
                       lat.reshape(bs, PAD_T, -1)[:, :ts]) + _state_out(*st))

    sp = [jnp.stack(a) for a in zip(*outs_p)]
    ss = [jnp.stack(a) for a in zip(*outs_s)]
    return (yp, ys[:, :ts], sp[0], sp[1], sp[2], sp[3], sp[4], sp[5], sp[6],
            ss[0], ss[1], ss[2], ss[3], ss[4], ss[5], ss[6])
```

```python
import functools

import jax
import jax.numpy as jnp
import numpy as np
from jax import lax
from jax.experimental import pallas as pl
from jax.experimental.pallas import tpu as pltpu

F32 = jnp.float32
BF16 = jnp.bfloat16

HEAD_DIM = 64
H_MLSTM = 4
H_FOX = 4
H_MLA = 8
MLSTM_CHUNK = 128
KV_LORA = 128
NOPE_DIM = 64
ROPE_DIM = 32
ROPE_THETA = 10000.0
LN_EPS = 1e-5
RMS_EPS = 1e-6

LANES = 128
SUBLANES = 8
VMEM_LIMIT = 56 * 1024 * 1024

C_M = 0
C_F = 1024
C_CQ = 1792
C_CKV = 2048
C_G = 2176
D_INP = 2304
G_KR = 0
G_MI = 32
G_MF = 36
G_FF = 40
G_ROWS = 16

PAD_T = 8
NEG_INF = float("-inf")
LOG2E = 1.4426950408889634
STAGE_LAG = 2
ATT_TQ = 256
ATT_TK = 512
DECODE_PAGES = 32
MLSTM_SEQS = 4


def _bf(x):
    return x.astype(BF16)


def _dot(a, b):
    return jnp.dot(a, b, preferred_element_type=F32)


def _dot_nt(a, b):
    return lax.dot_general(a, b, (((1,), (1,)), ((), ())), preferred_element_type=F32)


def _dot_tn(a, b):
    return lax.dot_general(a, b, (((0,), (0,)), ((), ())), preferred_element_type=F32)


def _split3(x):
    hi = _bf(x)
    r = x - hi.astype(F32)
    mid = _bf(r)
    lo = _bf(r - mid.astype(F32))
    return hi, mid, lo


def _cumsum_rows(tril_bf, x):
    hi, mid, lo = _split3(x)
    return _dot(tril_bf, hi) + _dot(tril_bf, mid) + _dot(tril_bf, lo)


def _cumsum_lanes(x, triu_bf):
    hi, mid, lo = _split3(x)
    return _dot(hi, triu_bf) + _dot(mid, triu_bf) + _dot(lo, triu_bf)


def _tri(n, lower):
    r = lax.broadcasted_iota(jnp.int32, (n, n), 0)
    c = lax.broadcasted_iota(jnp.int32, (n, n), 1)
    return jnp.where((r >= c) if lower else (r <= c), 1.0, 0.0).astype(BF16)


def _log_sigmoid(x):
    return jnp.minimum(x, 0.0) - jnp.log1p(jnp.exp(-jnp.abs(x)))


def _head_mask(width, h):
    lane = lax.broadcasted_iota(jnp.int32, (1, width), 1)
    return (lane // HEAD_DIM) == h


def _head_rmsnorm(x, gh, n_heads):
    width = n_heads * HEAD_DIM
    x2 = x * x
    inv = jnp.zeros_like(x)
    for h in range(n_heads):
        mk = _head_mask(width, h)
        ms = jnp.sum(jnp.where(mk, x2, 0.0), axis=1, keepdims=True) * (1.0 / HEAD_DIM)
        inv = jnp.where(mk, lax.rsqrt(ms + RMS_EPS), inv)
    return x * inv * gh


def _layernorm(x, g, b):
    mu = jnp.mean(x, axis=1, keepdims=True)
    xc = x - mu
    var = jnp.mean(xc * xc, axis=1, keepdims=True)
    return xc * lax.rsqrt(var + LN_EPS) * g + b


def _params(sem):
    return pltpu.CompilerParams(dimension_semantics=sem, vmem_limit_bytes=VMEM_LIMIT)


def _token_tile(n):
    for t in (512, 256, 128, 64, 32, 16, 8):
        if n % t == 0:
            return t
    raise ValueError(f"token count {n} is not a multiple of 8")


def _const_spec(shape):
    nd = len(shape)
    return pl.BlockSpec(shape, lambda *_: (0,) * nd)


def _inproj_kernel(x_ref, w_ref, gb_ref, zm_ref, zf_ref, zcq_ref, zckv_ref, zg_ref, kb_ref, vb_ref):
    xb = _bf(x_ref[...])
    for c in range(4):
        zm_ref[:, c * 256:(c + 1) * 256] = _dot(xb, w_ref[:, C_M + c * 256:C_M + (c + 1) * 256])
    for c in range(3):
        z = _dot(xb, w_ref[:, C_F + c * 256:C_F + (c + 1) * 256])
        zf_ref[:, c * 256:(c + 1) * 256] = z
        if c == 1:
            kb_ref[...] = _bf(z)
        if c == 2:
            vb_ref[...] = _bf(z)
    zcq_ref[...] = _dot(xb, w_ref[:, C_CQ:C_CQ + 256])
    last = _dot(xb, w_ref[:, C_CKV:D_INP])
    zckv_ref[...] = last[:, :KV_LORA]
    g = last[:, KV_LORA:] + gb_ref[...]
    lane = lax.broadcasted_iota(jnp.int32, g.shape, 1)
    zg_ref[...] = jnp.where((lane >= G_MF) & (lane < G_FF + H_FOX), _log_sigmoid(g), g)


def _inproj(x, w_perm, gbias):
    n, d = x.shape
    tm = _token_tile(n)
    row = lambda w: pl.BlockSpec((tm, w), lambda i: (i, 0))
    wf = H_FOX * HEAD_DIM
    widths = (1024, 768, 256, KV_LORA, LANES, wf, wf)
    dtypes = (F32,) * 5 + (BF16,) * 2
    return pl.pallas_call(
        _inproj_kernel,
        grid=(n // tm,),
        in_specs=[row(d), _const_spec(w_perm.shape), _const_spec(gbias.shape)],
        out_specs=[row(w) for w in widths],
        out_shape=[jax.ShapeDtypeStruct((n, w), dt) for w, dt in zip(widths, dtypes)],
        compiler_params=_params(("parallel",)),
        name="inproj",
    )(x, w_perm, gbias)


def _mlstm_kernel(q_ref, k_ref, v_ref, o_ref, zg_ref, gt_ref, c0_ref, n0_ref, m0_ref, gh_ref,
                  hm_ref, c1_ref, n1_ref, m1_ref, c_sc, n_sc, m_sc, *, chunk, n_valid, n_seq):
    ci = pl.program_id(1)

    W = H_MLSTM * HEAD_DIM

    @pl.when(ci == 0)
    def _():
        for s in range(n_seq):
            for h in range(H_MLSTM):
                rows = slice(h * HEAD_DIM, (h + 1) * HEAD_DIM)
                parts = [jnp.zeros((HEAD_DIM, HEAD_DIM), F32)] * H_MLSTM
                parts[h] = c0_ref[s, rows, :]
                c_sc[s, rows, :] = jnp.concatenate(parts, axis=1)
        n_sc[...] = n0_ref[...]
        m_sc[...] = m0_ref[...]

    for s in range(n_seq):
        _mlstm_chunk(q_ref.at[s], k_ref.at[s], v_ref.at[s], o_ref.at[s], zg_ref.at[s], gt_ref.at[s], gh_ref,
                     hm_ref.at[s], c_sc.at[s], n_sc.at[s], m_sc.at[s], chunk, n_valid)

    @pl.when(ci == pl.num_programs(1) - 1)
    def _():
        for s in range(n_seq):
            for h in range(H_MLSTM):
                rows = slice(h * HEAD_DIM, (h + 1) * HEAD_DIM)
                c1_ref[s, rows, :] = c_sc[s, rows, h * HEAD_DIM:(h + 1) * HEAD_DIM]
        n1_ref[...] = n_sc[...]
        m1_ref[...] = m_sc[...]


def _mlstm_chunk(q_ref, k_ref, v_ref, o_ref, zg_ref, gt_ref, gh_ref, hm_ref, c_sc, n_sc, m_sc, L, n_valid):
    W = H_MLSTM * HEAD_DIM
    q = q_ref[...]
    k = k_ref[...] * (HEAD_DIM ** -0.5)
    v = v_ref[...]
    zg = zg_ref[...]
    gt = gt_ref[...]
    valid_c = lax.broadcasted_iota(jnp.int32, (L, 1), 0) < n_valid
    valid_r = lax.broadcasted_iota(jnp.int32, (1, L), 1) < n_valid
    cum_c = _cumsum_rows(_tri(L, True), jnp.where(valid_c, zg, 0.0))
    cum_r = _cumsum_lanes(jnp.where(valid_r, gt, 0.0), _tri(L, False))
    row = lax.broadcasted_iota(jnp.int32, (L, L), 0)
    col = lax.broadcasted_iota(jnp.int32, (L, L), 1)
    tril = row >= col
    qb, kb, vb = _bf(q), _bf(k), _bf(v)
    cb = _bf(c_sc[...])
    n_prev = n_sc[...]

    h_all = jnp.zeros((L, W), F32)
    w_all = jnp.zeros((L, W), F32)
    cdec_all = jnp.zeros((1, W), F32)
    for h in range(H_MLSTM):
        mk = _head_mask(W, h)
        b_c = cum_c[:, G_MF + h:G_MF + h + 1]
        b_r = cum_r[H_MLSTM + h:H_MLSTM + h + 1, :]
        ig_c = jnp.where(valid_c, zg[:, G_MI + h:G_MI + h + 1], NEG_INF)
        ig_r = jnp.where(valid_r, gt[h:h + 1, :], NEG_INF)
        m_prev = m_sc[h:h + 1, 0:1]
        dlog = jnp.where(tril, b_c - b_r + ig_r, NEG_INF)
        inter = b_c + m_prev
        mt = jnp.maximum(inter, jnp.max(dlog, axis=1, keepdims=True))
        dmat = jnp.exp(dlog - mt)
        qh = jnp.where(mk, q, 0.0)
        qhb = _bf(qh)
        s = _dot_nt(qhb, kb) * dmat
        dec = jnp.exp(inter - mt)
        num = _dot(_bf(s), vb) + dec * _dot(qhb, cb)
        den = jnp.sum(s, axis=1, keepdims=True) + dec * jnp.sum(qh * n_prev, axis=1, keepdims=True)
        hh = num / jnp.maximum(jnp.abs(den), jnp.exp(-mt))
        h_all = jnp.where(mk, hh, h_all)
        m_new = mt[L - 1:L, :]
        b_last = b_c[L - 1:L, :]
        w_c = jnp.exp(b_last - b_c + ig_c - m_new)
        cdec = jnp.exp(b_last + m_prev - m_new)
        w_all = jnp.where(mk, w_c, w_all)
        cdec_all = jnp.where(mk, cdec, cdec_all)
        m_sc[h:h + 1, :] = jnp.broadcast_to(m_new, (1, LANES))

    kw = k * w_all
    r2 = lax.broadcasted_iota(jnp.int32, (W, W), 0) // HEAD_DIM
    c2 = lax.broadcasted_iota(jnp.int32, (W, W), 1) // HEAD_DIM
    c_sc[...] = jnp.where(r2 == c2, cdec_all * c_sc[...] + _dot_tn(_bf(kw), vb), 0.0)
    n_sc[...] = cdec_all * n_prev + jnp.sum(kw, axis=0, keepdims=True)

    hm_ref[...] = _head_rmsnorm(h_all, gh_ref[...], H_MLSTM) * jax.nn.sigmoid(o_ref[...])


def _mlstm(zm3, zg3, gt3, c0, n0, m0, gh, n_valid):
    b, t, _ = zm3.shape
    chunk = MLSTM_CHUNK if t % MLSTM_CHUNK == 0 else t
    nc = t // chunk
    w = H_MLSTM * HEAD_DIM
    ns = MLSTM_SEQS if nc == 1 else 1
    while b % ns:
        ns //= 2
    col = lambda j: pl.BlockSpec((ns, chunk, w), lambda bi, ci: (bi, ci, j))
    state = lambda shape: pl.BlockSpec((ns,) + shape, lambda bi, ci: (bi, 0, 0))
    return pl.pallas_call(
        functools.partial(_mlstm_kernel, chunk=chunk, n_valid=n_valid, n_seq=ns),
        grid=(b // ns, nc),
        in_specs=[col(0), col(1), col(2), col(3),
                  pl.BlockSpec((ns, chunk, LANES), lambda bi, ci: (bi, ci, 0)),
                  pl.BlockSpec((ns, G_ROWS, chunk), lambda bi, ci: (bi, 0, ci)),
                  state((w, HEAD_DIM)), state((1, w)), state((SUBLANES, LANES)),
                  _const_spec((1, w))],
        out_specs=[pl.BlockSpec((ns, chunk, w), lambda bi, ci: (bi, ci, 0)),
                   state((w, HEAD_DIM)), state((1, w)), state((SUBLANES, LANES))],
        out_shape=[jax.ShapeDtypeStruct((b, t, w), F32),
                   jax.ShapeDtypeStruct((b, w, HEAD_DIM), F32),
                   jax.ShapeDtypeStruct((b, 1, w), F32),
                   jax.ShapeDtypeStruct((b, SUBLANES, LANES), F32)],
        scratch_shapes=[pltpu.VMEM((ns, w, w), F32), pltpu.VMEM((ns, 1, w), F32),
                        pltpu.VMEM((ns, SUBLANES, LANES), F32)],
        compiler_params=_params(("parallel", "arbitrary")),
        name="mlstm",
    )(zm3, zm3, zm3, zm3, zg3, gt3, c0, n0, m0, gh)


def _fcum_kernel(gt_ref, fr_ref, cr_sc, *, chunk):
    L = chunk

    @pl.when(pl.program_id(1) == 0)
    def _():
        cr_sc[...] = jnp.zeros_like(cr_sc)

    cum_r = _cumsum_lanes(gt_ref[0], _tri(L, False)) + cr_sc[:, 0:1]
    fr_ref[0] = cum_r
    cr_sc[...] = jnp.broadcast_to(cum_r[:, L - 1:L], cr_sc.shape)


def _fcum(gt3):
    b, _, t = gt3.shape
    chunk = 512 if t % 512 == 0 else t
    return pl.pallas_call(
        functools.partial(_fcum_kernel, chunk=chunk),
        grid=(b, t // chunk),
        in_specs=[pl.BlockSpec((1, G_ROWS, chunk), lambda bi, ci: (bi, 0, ci))],
        out_specs=pl.BlockSpec((1, G_ROWS, chunk), lambda bi, ci: (bi, 0, ci)),
        out_shape=jax.ShapeDtypeStruct(gt3.shape, F32),
        scratch_shapes=[pltpu.VMEM((G_ROWS, LANES), F32)],
        compiler_params=_params(("parallel", "arbitrary")),
        name="fcum",
    )(gt3)


def _lane_tile(x, n):
    return x if n == 1 else jnp.concatenate([x] * n, axis=1)


def _staged(n, lag, *stages):
    for i in range(n + lag * (len(stages) - 1)):
        for s, stage in enumerate(stages):
            if 0 <= i - s * lag < n:
                stage(i - s * lag)


def _fox_kernel(q_ref, k_ref, v_ref, fr_ref, gh_ref, o_ref, qh_sc, acc_sc, m_sc, l_sc, *stage_sc, tq, tk):
    s_sc, p_sc, al_sc = stage_sc[0:H_FOX], stage_sc[H_FOX:2 * H_FOX], stage_sc[2 * H_FOX:]
    W = H_FOX * HEAD_DIM
    nt = tk // LANES
    qi = pl.program_id(1)
    n_full = (qi * tq) // tk
    q = q_ref[0]
    for h in range(H_FOX):
        qh_sc[h] = _bf(jnp.where(_head_mask(W, h), q, 0.0))
    acc_sc[...] = jnp.zeros_like(acc_sc)
    m_sc[...] = jnp.full_like(m_sc, NEG_INF)
    l_sc[...] = jnp.zeros_like(l_sc)

    def step(j, masked):
        off = pl.multiple_of(j * tk, tk)
        kb = k_ref[0, pl.ds(off, tk), :]
        vb = v_ref[0, pl.ds(off, tk), :]
        fk = fr_ref[0, :, pl.ds(off, tk)] * LOG2E
        if masked:
            qpos = qi * tq + lax.broadcasted_iota(jnp.int32, (tq, tk), 0)
            kpos = off + lax.broadcasted_iota(jnp.int32, (tq, tk), 1)
            causal = kpos <= qpos

        def scores(h):
            s_sc[h][...] = _dot_nt(qh_sc[h], kb)

        def softmax(h):
            t = s_sc[h][...] * (HEAD_DIM ** -0.5 * LOG2E) - fk[2 * H_MLSTM + h:2 * H_MLSTM + h + 1, :]
            if masked:
                t = jnp.where(causal, t, NEG_INF)
            m_prev = m_sc[h]
            m_new = jnp.maximum(m_prev, jnp.max(t, axis=1, keepdims=True))
            alpha = jnp.exp2(m_prev - m_new)
            p = jnp.exp2(t - _lane_tile(m_new, nt))
            l_sc[h] = alpha * l_sc[h] + jnp.sum(p, axis=1, keepdims=True)
            m_sc[h] = m_new
            al_sc[h][...] = alpha
            p_sc[h][...] = _bf(p)

        def values(h):
            acc_sc[h] = _lane_tile(al_sc[h][...], W // LANES) * acc_sc[h] + _dot(p_sc[h][...], vb)

        _staged(H_FOX, STAGE_LAG, scores, softmax, values)

    def full_step(j, carry):
        step(j, False)
        return carry

    lax.fori_loop(0, n_full, full_step, 0)
    step(n_full, True)

    out = jnp.zeros((tq, W), F32)
    for h in range(H_FOX):
        out = jnp.where(_head_mask(W, h), acc_sc[h] * _lane_tile(1.0 / l_sc[h], W // LANES), out)
    o_ref[0] = _head_rmsnorm(out, gh_ref[...], H_FOX)


def _fox_prompt(zf3, kb3, vb3, fr3, gh):
    b, t, _ = zf3.shape
    w = H_FOX * HEAD_DIM
    tq = ATT_TQ if t % ATT_TQ == 0 else t
    tk = ATT_TK if t % ATT_TK == 0 else t
    seq = lambda shape: pl.BlockSpec((1,) + shape, lambda bi, qi: (bi, 0, 0))
    return pl.pallas_call(
        functools.partial(_fox_kernel, tq=tq, tk=tk),
        grid=(b, t // tq),
        in_specs=[pl.BlockSpec((1, tq, w), lambda bi, qi: (bi, qi, 0)),
                  seq((t, w)), seq((t, w)), seq((G_ROWS, t)), _const_spec((1, w))],
        out_specs=pl.BlockSpec((1, tq, w), lambda bi, qi: (bi, qi, 0)),
        out_shape=jax.ShapeDtypeStruct((b, t, w), F32),
        scratch_shapes=([pltpu.VMEM((H_FOX, tq, w), BF16), pltpu.VMEM((H_FOX, tq, w), F32),
                         pltpu.VMEM((H_FOX, tq, LANES), F32), pltpu.VMEM((H_FOX, tq, LANES), F32)]
                        + [pltpu.VMEM((tq, tk), F32)] * H_FOX + [pltpu.VMEM((tq, tk), BF16)] * H_FOX
                        + [pltpu.VMEM((tq, LANES), F32)] * H_FOX),
        compiler_params=_params(("parallel", "arbitrary")),
        name="fox_prompt",
    )(zf3, kb3, vb3, fr3, gh)


def _rope_swap(x, group):
    w = x.shape[1]
    half = ROPE_DIM // 2
    lane = lax.broadcasted_iota(jnp.int32, x.shape, 1)
    first = (lane % group) < half
    return jnp.where(first, pltpu.roll(x, w - half, 1), pltpu.roll(x, half, 1))


def _mla_pre_kernel(zcq_ref, zckv_ref, zg_ref, cs_ref, sn_ref, gcq_ref, wn_ref, wr_ref, gckv_ref, wuk_ref, sel_ref,
                    q_ref, lat_ref, latp_ref):
    zcq = zcq_ref[...]
    cqn = zcq * lax.rsqrt(jnp.mean(zcq * zcq, axis=1, keepdims=True) + RMS_EPS) * gcq_ref[...]
    cqb = _bf(cqn)
    qn = _dot(cqb, wn_ref[...])
    qr = _dot(cqb, wr_ref[...])
    cs = cs_ref[...]
    sn = sn_ref[...]
    qrot = qr * cs + _rope_swap(qr, ROPE_DIM) * sn
    qlat = _dot(_bf(qn), wuk_ref[...])
    qrope = _dot(_bf(qrot), sel_ref[...])
    for h in range(H_MLA):
        q_ref[0, h, :, 0:LANES] = qlat[:, h * LANES:(h + 1) * LANES].astype(q_ref.dtype)
        q_ref[0, h, :, LANES:2 * LANES] = qrope[:, h * LANES:(h + 1) * LANES].astype(q_ref.dtype)

    zckv = zckv_ref[...]
    ckvn = zckv * lax.rsqrt(jnp.mean(zckv * zckv, axis=1, keepdims=True) + RMS_EPS) * gckv_ref[...]
    zg = zg_ref[...]
    krot = zg * cs[:, :LANES] + _rope_swap(zg, ROPE_DIM) * sn[:, :LANES]
    lane = lax.broadcasted_iota(jnp.int32, krot.shape, 1)
    krz = jnp.where(lane < ROPE_DIM, krot, jnp.where(lane == LANES - 1, 1.0, 0.0))
    lat_ref[:, 0:KV_LORA] = ckvn
    lat_ref[:, KV_LORA:KV_LORA + ROPE_DIM] = krot[:, 0:ROPE_DIM]
    latp_ref[:, 0:KV_LORA] = _bf(ckvn)
    latp_ref[:, KV_LORA:2 * KV_LORA] = _bf(krz)


def _mla_pre(zcq, zckv, zg, cs, sn, lw, bq, tq, q_dtype):
    n = zcq.shape[0]
    tm = _token_tile(min(n, tq))
    tm = min(tm, tq)
    nt_tab = cs.shape[0] // tm
    nt_q = tq // tm
    row = lambda w: pl.BlockSpec((tm, w), lambda i: (i, 0))
    tab = pl.BlockSpec((tm, 2 * LANES), lambda i: (i % nt_tab, 0))
    consts = (lw["g_cq"], lw["w_uq_n"], lw["w_uq_r"], lw["g_ckv"], lw["wuk_bd"], lw["sel"])
    return pl.pallas_call(
        _mla_pre_kernel,
        grid=(n // tm,),
        in_specs=[row(256), row(KV_LORA), row(LANES), tab, tab] + [_const_spec(c.shape) for c in consts],
        out_specs=[pl.BlockSpec((1, H_MLA, tm, 2 * LANES), lambda i: (i // nt_q, 0, i % nt_q, 0)),
                   row(KV_LORA + ROPE_DIM), row(2 * LANES)],
        out_shape=[jax.ShapeDtypeStruct((bq, H_MLA, tq, 2 * LANES), q_dtype),
                   jax.ShapeDtypeStruct((n, KV_LORA + ROPE_DIM), F32),
                   jax.ShapeDtypeStruct((n, 2 * LANES), BF16)],
        compiler_params=_params(("parallel",)),
        name="mla_pre",
    )(zcq, zckv, zg, cs, sn, *consts)


def _mla_out(acc, inv_l, wuv_ref, gh, rows):
    out = jnp.zeros((rows, H_MLA * HEAD_DIM), F32)
    for h in range(H_MLA):
        o_h = acc[h * rows:(h + 1) * rows] * inv_l[h * rows:(h + 1) * rows]
        out = out + _dot(_bf(o_h), wuv_ref[h])
    return _head_rmsnorm(out, gh, H_MLA)


def _mla_kernel(q_ref, k_ref, wuv_ref, gh_ref, o_ref, acc_sc, m_sc, *stage_sc, tq, tk):
    s_sc, p_sc, al_sc = stage_sc[0:H_MLA], stage_sc[H_MLA:2 * H_MLA], stage_sc[2 * H_MLA:]
    nt = tk // LANES
    scale2 = (NOPE_DIM + ROPE_DIM) ** -0.5 * LOG2E
    qi = pl.program_id(1)
    n_full = (qi * tq) // tk
    acc_sc[...] = jnp.zeros_like(acc_sc)
    m_sc[...] = jnp.full_like(m_sc, NEG_INF)

    def step(j, masked):
        off = pl.multiple_of(j * tk, tk)
        kb = k_ref[0, pl.ds(off, tk), :]
        if masked:
            qpos = qi * tq + lax.broadcasted_iota(jnp.int32, (tq, tk), 0)
            kpos = off + lax.broadcasted_iota(jnp.int32, (tq, tk), 1)
            causal = kpos <= qpos
        def scores(h):
            s_sc[h][...] = _dot_nt(q_ref[0, h], kb)

        def softmax(h):
            s = s_sc[h][...]
            if masked:
                s = jnp.where(causal, s, NEG_INF)
            m_prev = m_sc[h]
            m_new = jnp.maximum(m_prev, jnp.max(s, axis=1, keepdims=True))
            al_sc[h][...] = jnp.exp2((m_prev - m_new) * scale2)
            p_sc[h][...] = _bf(jnp.exp2((s - _lane_tile(m_new, nt)) * scale2))
            m_sc[h] = m_new

        def values(h):
            acc_sc[h] = _lane_tile(al_sc[h][...], 2) * acc_sc[h] + _dot(p_sc[h][...], kb)

        _staged(H_MLA, STAGE_LAG, scores, softmax, values)

    def full_step(j, carry):
        step(j, False)
        return carry

    lax.fori_loop(0, n_full, full_step, 0)
    step(n_full, True)

    out = jnp.zeros((tq, H_MLA * HEAD_DIM), F32)
    for h in range(H_MLA):
        acc = acc_sc[h]
        o_h = acc[:, :KV_LORA] * (1.0 / acc[:, 2 * LANES - 1:2 * LANES])
        out = out + _dot(_bf(o_h), wuv_ref[h])
    o_ref[0] = _head_rmsnorm(out, gh_ref[...], H_MLA)


def _mla_prompt(qp, latp3, wuv_pad, gh):
    b, _, t, _ = qp.shape
    tq = ATT_TQ if t % ATT_TQ == 0 else t
    tk = ATT_TK if t % ATT_TK == 0 else t
    w = H_MLA * HEAD_DIM
    return pl.pallas_call(
        functools.partial(_mla_kernel, tq=tq, tk=tk),
        grid=(b, t // tq),
        in_specs=[pl.BlockSpec((1, H_MLA, tq, 2 * LANES), lambda bi, qi: (bi, 0, qi, 0)),
                  pl.BlockSpec((1, t, 2 * LANES), lambda bi, qi: (bi, 0, 0)),
                  _const_spec(wuv_pad.shape), _const_spec((1, w))],
        out_specs=pl.BlockSpec((1, tq, w), lambda bi, qi: (bi, qi, 0)),
        out_shape=jax.ShapeDtypeStruct((b, t, w), F32),
        scratch_shapes=([pltpu.VMEM((H_MLA, tq, 2 * LANES), F32), pltpu.VMEM((H_MLA, tq, LANES), F32)]
                        + [pltpu.VMEM((tq, tk), F32)] * H_MLA + [pltpu.VMEM((tq, tk), BF16)] * H_MLA
                        + [pltpu.VMEM((tq, LANES), F32)] * H_MLA),
        compiler_params=_params(("parallel", "arbitrary")),
        name="mla_prompt",
    )(qp, latp3, wuv_pad, gh)


def _decode_kernel(pt_ref, kt_hbm, vt_hbm, lt_hbm, lf_hbm, kn_ref, vn_ref, ln_ref, gtn_ref, qf_ref, qa_ref,
                   wuv_ref, ghf_ref, gha_ref, of_ref, oa_ref,
                   k_buf, v_buf, l_buf, f_buf, sems, lat16, facc, fm, fl, aacc, am, al, carry,
                   *, n_group, layer, n_new):
    G = n_group
    bi = pl.program_id(0)
    j = pl.program_id(1)
    nj = pl.num_programs(1)
    step = bi * nj + j
    last_step = pl.num_programs(0) * nj - 1
    WF = H_FOX * HEAD_DIM
    RF = H_FOX * PAD_T
    RA = H_MLA * PAD_T

    def page_copies(s, slot):
        b_s = s // nj
        first = (s % nj) * 2 * G + slot * G
        out = []
        for g in range(G):
            page = pt_ref[b_s, first + g]
            out.append(pltpu.make_async_copy(kt_hbm.at[layer, page], k_buf.at[slot, g], sems.at[slot, 0]))
            out.append(pltpu.make_async_copy(vt_hbm.at[layer, page], v_buf.at[slot, g], sems.at[slot, 1]))
            out.append(pltpu.make_async_copy(lt_hbm.at[layer, page], l_buf.at[slot, g], sems.at[slot, 2]))
            out.append(pltpu.make_async_copy(lf_hbm.at[layer, page],
                                             f_buf.at[slot, pl.ds(g * SUBLANES, H_FOX), :], sems.at[slot, 3]))
        return out

    @pl.when(step == 0)
    def _():
        for slot in range(2):
            for g in range(G):
                f_buf[slot, g * SUBLANES + H_FOX:(g + 1) * SUBLANES, :] = jnp.zeros((SUBLANES - H_FOX, LANES), F32)
        lat16[...] = jnp.zeros_like(lat16)
        for slot in range(2):
            for c in page_copies(step, slot):
                c.start()

    @pl.when(j == 0)
    def _():
        facc[...] = jnp.zeros_like(facc)
        fm[...] = jnp.full_like(fm, NEG_INF)
        fl[...] = jnp.zeros_like(fl)
        aacc[...] = jnp.zeros_like(aacc)
        am[...] = jnp.full_like(am, NEG_INF)
        al[...] = jnp.zeros_like(al)
        carry[...] = jnp.zeros_like(carry)

    q = qf_ref[0]
    qbd = jnp.concatenate([_bf(jnp.where(_head_mask(WF, h), q, 0.0)) for h in range(H_FOX)], axis=0)
    qa = _bf(qa_ref[0].reshape(RA, 2 * LANES))
    triu = _tri(LANES, False)
    scale_f = HEAD_DIM ** -0.5 * LOG2E
    scale_a = (NOPE_DIM + ROPE_DIM) ** -0.5 * LOG2E

    def like_keys(m, width):
        return _lane_tile(m, width // LANES) if width % LANES == 0 else m[:, :width]

    def merge(s_f, s_a, pv_f, pv_a):
        width = s_f.shape[1]
        m_prev = fm[...]
        m_new = jnp.maximum(m_prev, jnp.max(s_f, axis=1, keepdims=True))
        alpha = jnp.exp2(m_prev - m_new)
        p = jnp.exp2(s_f - like_keys(m_new, width))
        fl[...] = alpha * fl[...] + jnp.sum(p, axis=1, keepdims=True)
        fm[...] = m_new
        facc[...] = _lane_tile(alpha, WF // LANES) * facc[...] + pv_f(_bf(p))
        m_prev = am[...]
        m_new = jnp.maximum(m_prev, jnp.max(s_a, axis=1, keepdims=True))
        alpha = jnp.exp2((m_prev - m_new) * scale_a)
        p = jnp.exp2((s_a - like_keys(m_new, width)) * scale_a)
        al[...] = alpha * al[...] + jnp.sum(p, axis=1, keepdims=True)
        am[...] = m_new
        aacc[...] = alpha * aacc[...] + pv_a(_bf(p))

    def head_rows(cum, g0):
        return jnp.concatenate([jnp.broadcast_to(cum[g0 + h:g0 + h + 1, :], (PAD_T, cum.shape[1]))
                                for h in range(H_FOX)], axis=0)

    def block(kts, vts, lts, lf8, kbuf):
        n = len(kts)
        cs = _cumsum_lanes(lf8, triu)
        tot = jnp.broadcast_to(cs[:, LANES - 1:LANES], cs.shape)
        base = jnp.concatenate([carry[...]] * n, axis=0)
        if n > 1:
            r = lax.broadcasted_iota(jnp.int32, (n * SUBLANES, n * SUBLANES), 0)
            c = lax.broadcasted_iota(jnp.int32, (n * SUBLANES, n * SUBLANES), 1)
            earlier = jnp.where((r % SUBLANES == c % SUBLANES) & (c // SUBLANES < r // SUBLANES), 1.0, 0.0)
            base = base + _cumsum_rows(earlier.astype(BF16), tot)
        last = slice((n - 1) * SUBLANES, n * SUBLANES)
        carry[...] = base[last] + tot[last]
        cum = (cs + base) * LOG2E
        s_f, s_a = [], []
        for g in range(n):
            s_f.append(_dot(qbd, _bf(kts[g][...])) * scale_f - head_rows(cum, g * SUBLANES))
            kbuf[g, 0:KV_LORA + ROPE_DIM, :] = _bf(lts[g][...])
            s_a.append(_dot(qa, kbuf[g]))

        def pv_f(p):
            return sum(_dot_nt(p[:, g * LANES:(g + 1) * LANES], _bf(vts[g][...])) for g in range(n))

        def pv_a(p):
            return sum(_dot_nt(p[:, g * LANES:(g + 1) * LANES], kbuf[g, 0:KV_LORA, :]) for g in range(n))

        merge(jnp.concatenate(s_f, axis=1), jnp.concatenate(s_a, axis=1), pv_f, pv_a)

    def new_token_block():
        kn, vn, ln = _bf(kn_ref[0]), _bf(vn_ref[0]), ln_ref[0]
        valid = lax.broadcasted_iota(jnp.int32, (1, PAD_T), 1) < n_new
        cs = _cumsum_lanes(jnp.where(valid, gtn_ref[0], 0.0), _tri(PAD_T, False))
        cum = (cs[2 * H_MLSTM:2 * H_MLSTM + H_FOX] + carry[0:H_FOX, 0:PAD_T]) * LOG2E
        s_f = _dot_nt(qbd, kn) * scale_f - head_rows(cum, 0)
        s_a = _dot_nt(qa, ln)
        ok_f = lax.broadcasted_iota(jnp.int32, s_f.shape, 1) <= lax.broadcasted_iota(jnp.int32, s_f.shape, 0) % PAD_T
        ok_a = lax.broadcasted_iota(jnp.int32, s_a.shape, 1) <= lax.broadcasted_iota(jnp.int32, s_a.shape, 0) % PAD_T
        merge(jnp.where(ok_f, s_f, NEG_INF), jnp.where(ok_a, s_a, NEG_INF),
              lambda p: _dot(p, vn), lambda p: _dot(p, ln[:, :KV_LORA]))

    def half_step(slot):
        for c in page_copies(step, slot):
            c.wait()
        block([k_buf.at[slot, g] for g in range(G)], [v_buf.at[slot, g] for g in range(G)],
              [l_buf.at[slot, g] for g in range(G)], f_buf[slot], lat16.at[slot])

    nxt = jnp.minimum(step + 1, last_step)
    half_step(0)
    for c in page_copies(nxt, 0):
        c.start()
    half_step(1)
    for c in page_copies(nxt, 1):
        c.start()

    @pl.when(step == last_step)
    def _():
        for slot in range(2):
            for c in page_copies(step, slot):
                c.wait()

    @pl.when(j == nj - 1)
    def _():
        new_token_block()
        accf = facc[...] * _lane_tile(1.0 / fl[...], WF // LANES)
        of = jnp.zeros((PAD_T, WF), F32)
        for h in range(H_FOX):
            of = jnp.where(_head_mask(WF, h), accf[h * PAD_T:(h + 1) * PAD_T], of)
        of_ref[0] = _head_rmsnorm(of, ghf_ref[...], H_FOX)
        oa_ref[0] = _mla_out(aacc[...], 1.0 / al[...], wuv_ref, gha_ref[...], PAD_T)


def _decode(page_table, layer, caches, zf3, latp3, gt3, qa, wuv_pad, ghf, gha, n_new):
    b, n_pages = page_table.shape
    assert n_pages % 2 == 0, "the two-slot page pipeline needs an even number of pages per sequence"
    G = DECODE_PAGES
    while n_pages % (2 * G):
        G //= 2
    kt, vt, lt, lf = caches
    page = kt.shape[-1]

    def seq_spec(arr):
        nd = arr.ndim
        return pl.BlockSpec((1,) + arr.shape[1:], lambda bi, ji, pt: (bi,) + (0,) * (nd - 1))

    def const(arr):
        nd = arr.ndim
        return pl.BlockSpec(arr.shape, lambda bi, ji, pt: (0,) * nd)

    wf = H_FOX * HEAD_DIM
    wa = H_MLA * HEAD_DIM
    zcol = lambda c: pl.BlockSpec((1, PAD_T, wf), lambda bi, ji, pt: (bi, 0, c))
    in_specs = [pl.BlockSpec(memory_space=pl.ANY)] * 4 + [zcol(1), zcol(2), seq_spec(latp3), seq_spec(gt3),
                                                           zcol(0), seq_spec(qa)]
    operands = [kt, vt, lt, lf, zf3, zf3, latp3, gt3, zf3, qa]
    for arr in (wuv_pad, ghf, gha):
        in_specs.append(const(arr))
        operands.append(arr)
    grid_spec = pltpu.PrefetchScalarGridSpec(
        num_scalar_prefetch=1,
        grid=(b, n_pages // (2 * G)),
        in_specs=in_specs,
        out_specs=[pl.BlockSpec((1, PAD_T, wf), lambda bi, ji, pt: (bi, 0, 0)),
                   pl.BlockSpec((1, PAD_T, wa), lambda bi, ji, pt: (bi, 0, 0))],
        scratch_shapes=[pltpu.VMEM((2, G) + kt.shape[2:], F32), pltpu.VMEM((2, G) + vt.shape[2:], F32),
                        pltpu.VMEM((2, G) + lt.shape[2:], F32), pltpu.VMEM((2, G * SUBLANES, page), F32),
                        pltpu.SemaphoreType.DMA((2, 4)),
                        pltpu.VMEM((2, G, 2 * LANES, page), BF16),
                        pltpu.VMEM((H_FOX * PAD_T, wf), F32), pltpu.VMEM((H_FOX * PAD_T, LANES), F32),
                        pltpu.VMEM((H_FOX * PAD_T, LANES), F32),
                        pltpu.VMEM((H_MLA * PAD_T, KV_LORA), F32), pltpu.VMEM((H_MLA * PAD_T, LANES), F32),
                        pltpu.VMEM((H_MLA * PAD_T, LANES), F32),
                        pltpu.VMEM((SUBLANES, LANES), F32)])
    return pl.pallas_call(
        functools.partial(_decode_kernel, n_group=G, layer=layer, n_new=n_new),
        grid_spec=grid_spec,
        out_shape=[jax.ShapeDtypeStruct((b, PAD_T, wf), F32), jax.ShapeDtypeStruct((b, PAD_T, wa), F32)],
        compiler_params=_params(("arbitrary", "arbitrary")),
        name="decode",
    )(page_table, *operands)


def _out_ffn_kernel(x_ref, hm_ref, hf_ref, ha_ref, wom_ref, wof_ref, woa_ref, g1_ref, b1_ref,
                    wg_ref, wu_ref, wd_ref, g2_ref, b2_ref, y_ref, acc_sc, *, alpha, n_chunks):
    mix = (_dot(_bf(hm_ref[...]), wom_ref[...]) + _dot(_bf(hf_ref[...]), wof_ref[...])
           + _dot(_bf(ha_ref[...]), woa_ref[...]))
    x1 = _layernorm(alpha * x_ref[...] + mix, g1_ref[...], b1_ref[...])
    x1b = _bf(x1)
    acc_sc[...] = jnp.zeros_like(acc_sc)

    for c in range(n_chunks):
        cols = slice(c * FF_CHUNK, (c + 1) * FF_CHUNK)
        g = _dot(x1b, wg_ref[:, cols])
        u = _dot(x1b, wu_ref[:, cols])
        acc_sc[...] += _dot(_bf(g * jax.nn.sigmoid(g) * u), wd_ref[cols, :])
    y_ref[...] = _layernorm(alpha * x1 + acc_sc[...], g2_ref[...], b2_ref[...])


def _out_ffn(x, hm, hf, ha, lw, alpha):
    n, d = x.shape
    tm = _token_tile(n)
    row = lambda w: pl.BlockSpec((tm, w), lambda i: (i, 0))
    consts = (lw["wo_m"], lw["wo_f"], lw["wo_a"], lw["ln1_g"], lw["ln1_b"],
              lw["wg"], lw["wu"], lw["wd"], lw["ln2_g"], lw["ln2_b"])

    def resident(c):
        nd = c.ndim
        return pl.BlockSpec(c.shape, lambda i: (0,) * nd, pipeline_mode=pl.Buffered(1))

    return pl.pallas_call(
        functools.partial(_out_ffn_kernel, alpha=alpha, n_chunks=lw["wg"].shape[1] // FF_CHUNK),
        grid=(n // tm,),
        in_specs=[row(d), row(hm.shape[1]), row(hf.shape[1]), row(ha.shape[1])] + [resident(c) for c in consts],
        out_specs=row(d),
        out_shape=jax.ShapeDtypeStruct((n, d), F32),
        scratch_shapes=[pltpu.VMEM((tm, d), F32)],
        compiler_params=_params(("parallel",)),
        name="out_ffn",
    )(x, hm, hf, ha, *consts)


FF_CHUNK = 256


def _prep_layer(l, w_in, b_mi, b_mf, b_ff, g_cq, w_uq, g_ckv, w_uk, w_uv, g_head, w_out,
                ln1_g, ln1_b, w_gate, w_up, w_down, ln2_g, ln2_b):
    d = w_in.shape[1]
    wm = H_MLSTM * HEAD_DIM
    wf = H_FOX * HEAD_DIM
    q_lora = g_cq.shape[1]
    split = (wm,) * 4 + (H_MLSTM, H_MLSTM) + (wf,) * 3 + (H_FOX,) + (q_lora, KV_LORA, ROPE_DIM)
    at = [int(i) for i in np.cumsum(split)[:-1]]
    mq, mk, mv, mo, mi, mf, fq, fk, fv, ff, cq, ckv, kr = jnp.split(w_in[l], at, axis=1)
    pad = jnp.zeros((d, LANES - ROPE_DIM - 2 * H_MLSTM - H_FOX), F32)
    w_perm = _bf(jnp.concatenate([mq, mk, mv, mo, fq, fk, fv, cq, ckv, kr, mi, mf, ff, pad], axis=1))
    gbias = jnp.zeros((1, LANES), F32)
    gbias = gbias.at[0, G_MI:G_MI + H_MLSTM].set(b_mi[l]).at[0, G_MF:G_MF + H_MLSTM].set(b_mf[l])
    gbias = gbias.at[0, G_FF:G_FF + H_FOX].set(b_ff[l])
    uq = w_uq[l].reshape(q_lora, H_MLA, NOPE_DIM + ROPE_DIM)
    eye = jnp.eye(H_MLA, dtype=F32)
    sel = np.zeros((H_MLA * ROPE_DIM, H_MLA * LANES), np.float32)
    for h in range(H_MLA):
        sel[h * ROPE_DIM + np.arange(ROPE_DIM), h * LANES + np.arange(ROPE_DIM)] = 1.0
    assert w_gate.shape[2] % FF_CHUNK == 0
    return dict(
        w_perm=w_perm, gbias=gbias,
        g_cq=g_cq[l][None], g_ckv=g_ckv[l][None],
        w_uq_n=_bf(uq[:, :, :NOPE_DIM].reshape(q_lora, H_MLA * NOPE_DIM)),
        w_uq_r=_bf(uq[:, :, NOPE_DIM:].reshape(q_lora, H_MLA * ROPE_DIM)),
        wuk_bd=_bf(jnp.einsum("chd,hg->hdgc", w_uk[l], eye).reshape(H_MLA * NOPE_DIM, H_MLA * KV_LORA)),
        sel=jnp.asarray(sel, BF16),
        wuv_pad=_bf(jnp.einsum("chd,hg->hcgd", w_uv[l], eye).reshape(H_MLA, KV_LORA, H_MLA * HEAD_DIM)),
        gh_m=g_head[l][None, :wm], gh_f=g_head[l][None, wm:wm + wf], gh_a=g_head[l][None, wm + wf:],
        wo_m=_bf(w_out[l][:wm]), wo_f=_bf(w_out[l][wm:wm + wf]), wo_a=_bf(w_out[l][wm + wf:]),
        ln1_g=ln1_g[l][None], ln1_b=ln1_b[l][None], ln2_g=ln2_g[l][None], ln2_b=ln2_b[l][None],
        wg=_bf(w_gate[l]), wu=_bf(w_up[l]), wd=_bf(w_down[l]),
    )


def _rope_tables(pos):
    half = ROPE_DIM // 2
    freq = ROPE_THETA ** (-jnp.arange(half, dtype=F32) * (2.0 / ROPE_DIM))
    ang = pos.astype(F32)[:, None] * freq[None, :]
    cos, sin = jnp.cos(ang), jnp.sin(ang)
    reps = 2 * LANES // ROPE_DIM
    return (jnp.tile(jnp.concatenate([cos, cos], axis=1), (1, reps)),
            jnp.tile(jnp.concatenate([-sin, sin], axis=1), (1, reps)))


def _layer_common(x3, lw, cs, sn, c0, n0, m0, n_valid, mla_bq, mla_tq, q_dtype):
    b, t, d = x3.shape
    zm, zf, zcq, zckv, zg, kb, vb = _inproj(x3.reshape(b * t, d), lw["w_perm"], lw["gbias"])
    zg3 = zg.reshape(b, t, LANES)
    gt3 = jnp.swapaxes(zg3[:, :, G_MI:G_MI + G_ROWS], 1, 2)
    hm, c1, n1, m1 = _mlstm(zm.reshape(b, t, -1), zg3, gt3, c0, n0, m0, lw["gh_m"], n_valid)
    qp, lat, latp = _mla_pre(zcq, zckv, zg, cs, sn, lw, mla_bq, mla_tq, q_dtype)
    return zf, (kb, vb), zg3, gt3, hm, (c1, n1, m1), qp, lat, latp


def _state_out(c1, n1, m1):
    b = c1.shape[0]
    return (c1.reshape(b, H_MLSTM, HEAD_DIM, HEAD_DIM), n1.reshape(b, H_MLSTM, HEAD_DIM), m1[:, :H_MLSTM, 0])


def kernel(x_prompt, x_sample, cache_fox_k, cache_fox_v, cache_fox_logf, cache_mla_latent, state_mlstm_C, state_mlstm_n, state_mlstm_m, page_table, w_in, b_mlstm_i, b_mlstm_f, b_fox_f, g_cq, w_uq, g_ckv, w_uk, w_uv, g_head, w_out, ln1_g, ln1_b, w_gate, w_up, w_down, ln2_g, ln2_b):
    depth = w_in.shape[0]
    bp, tp, d = x_prompt.shape
    bs, ts, _ = x_sample.shape
    n_pages = page_table.shape[1]
    page = cache_fox_k.shape[2]
    past = n_pages * page
    alpha = (2 * depth) ** 0.25
    wm = H_MLSTM * HEAD_DIM
    wf = H_FOX * HEAD_DIM

    pools = (jnp.transpose(cache_fox_k, (0, 1, 3, 4, 2)).reshape(depth, -1, wf, page),
             jnp.transpose(cache_fox_v, (0, 1, 3, 4, 2)).reshape(depth, -1, wf, page),
             jnp.transpose(cache_mla_latent, (0, 1, 3, 2)),
             jnp.transpose(cache_fox_logf, (0, 1, 3, 2)))

    cs_p, sn_p = _rope_tables(jnp.arange(tp, dtype=jnp.int32))
    cs_s, sn_s = _rope_tables(past + jnp.arange(PAD_T, dtype=jnp.int32))
    cs_s, sn_s = jnp.tile(cs_s, (bs, 1)), jnp.tile(sn_s, (bs, 1))

    yp = x_prompt
    ys = jnp.pad(x_sample, ((0, 0), (0, PAD_T - ts), (0, 0)))
    outs_p, outs_s = [], []
    for l in range(depth):
        lw = _prep_layer(l, w_in, b_mlstm_i, b_mlstm_f, b_fox_f, g_cq, w_uq, g_ckv, w_uk, w_uv, g_head, w_out,
                         ln1_g, ln1_b, w_gate, w_up, w_down, ln2_g, ln2_b)

        c0 = jnp.zeros((bp, wm, HEAD_DIM), F32)
        n0 = jnp.zeros((bp, 1, wm), F32)
        m0 = jnp.zeros((bp, SUBLANES, LANES), F32)
        zf, (kb, vb), zg3, gt3, hm, st, qp, lat, latp = _layer_common(yp, lw, cs_p, sn_p, c0, n0, m0, tp, bp, tp, BF16)
        fr3 = _fcum(gt3)
        zf3 = zf.reshape(bp, tp, -1)
        hf = _fox_prompt(zf3, kb.reshape(bp, tp, -1), vb.reshape(bp, tp, -1), fr3, lw["gh_f"])
        ha = _mla_prompt(qp, latp.reshape(bp, tp, -1), lw["wuv_pad"], lw["gh_a"])
        yp = _out_ffn(yp.reshape(bp * tp, d), hm.reshape(bp * tp, -1), hf.reshape(bp * tp, -1),
                      ha.reshape(bp * tp, -1), lw, alpha).reshape(bp, tp, d)
        outs_p.append((zf3[:, :, wf:2 * wf].reshape(bp, tp, H_FOX, HEAD_DIM),
                       zf3[:, :, 2 * wf:].reshape(bp, tp, H_FOX, HEAD_DIM),
                       zg3[:, :, G_FF:G_FF + H_FOX],
                       lat.reshape(bp, tp, -1)) + _state_out(*st))

        c0 = state_mlstm_C[l].reshape(bs, wm, HEAD_DIM)
        n0 = state_mlstm_n[l].reshape(bs, 1, wm)
        m0 = jnp.broadcast_to(jnp.pad(state_mlstm_m[l], ((0, 0), (0, SUBLANES - H_MLSTM)))[:, :, None],
                              (bs, SUBLANES, LANES))
        zf, _, zg3, gt3, hm, st, qa, lat, latp = _layer_common(ys, lw, cs_s, sn_s, c0, n0, m0, ts, 1, bs * PAD_T, F32)
        zf3 = zf.reshape(bs, PAD_T, -1)
        qa4 = jnp.swapaxes(qa.reshape(H_MLA, bs, PAD_T, 2 * LANES), 0, 1)
        hf, ha = _decode(page_table, l, pools, zf3, latp.reshape(bs, PAD_T, -1), gt3, qa4,
                         lw["wuv_pad"], lw["gh_f"], lw["gh_a"], ts)
        ys = _out_ffn(ys.reshape(bs * PAD_T, d), hm.reshape(bs * PAD_T, -1), hf.reshape(bs * PAD_T, -1),
                      ha.reshape(bs * PAD_T, -1), lw, alpha).reshape(bs, PAD_T, d)
        outs_s.append((zf3[:, :ts, wf:2 * wf].reshape(bs, ts, H_FOX, HEAD_DIM),
                       zf3[:, :ts, 2 * wf:].reshape(bs, ts, H_FOX, HEAD_DIM),
                       zg3[:, :ts, G_FF:G_FF + H_FOX],
                       lat.reshape(bs, PAD_T, -1)[:, :ts]) + _state_out(*st))

    sp = [jnp.stack(a) for a in zip(*outs_p)]
    ss = [jnp.stack(a) for a in zip(*outs_s)]
    return (yp, ys[:, :ts], sp[0], sp[1], sp[2], sp[3], sp[4], sp[5], sp[6],
            ss[0], ss[1], ss[2], ss[3], ss[4], ss[5], ss[6])
```

```python
import functools

import jax
import jax.numpy as jnp
import numpy as np
from jax import lax
from jax.experimental import pallas as pl
from jax.experimental.pallas import tpu as pltpu

F32 = jnp.float32
BF16 = jnp.bfloat16

HEAD_DIM = 64
H_MLSTM = 4
H_FOX = 4
H_MLA = 8
MLSTM_CHUNK = 256
KV_LORA = 128
NOPE_DIM = 64
ROPE_DIM = 32
ROPE_THETA = 10000.0
LN_EPS = 1e-5
RMS_EPS = 1e-6

LANES = 128
SUBLANES = 8
VMEM_LIMIT = 56 * 1024 * 1024

C_M = 0
C_F = 1024
C_CQ = 1792
C_CKV = 2048
C_G = 2176
D_INP = 2304
G_KR = 0
G_MI = 32
G_MF = 36
G_FF = 40
G_ROWS = 16

PAD_T = 8
NEG_INF = float("-inf")
LOG2E = 1.4426950408889634
STAGE_LAG = 1
ATT_TQ = 512
ATT_TK = 512
DECODE_PAGES = 32
MLSTM_SEQS = 4


def _bf(x):
    return x.astype(BF16)


def _dot(a, b):
    return jnp.dot(a, b, preferred_element_type=F32)


def _dot_nt(a, b):
    return lax.dot_general(a, b, (((1,), (1,)), ((), ())), preferred_element_type=F32)


def _dot_tn(a, b):
    return lax.dot_general(a, b, (((0,), (0,)), ((), ())), preferred_element_type=F32)


def _split3(x):
    hi = _bf(x)
    r = x - hi.astype(F32)
    mid = _bf(r)
    lo = _bf(r - mid.astype(F32))
    return hi, mid, lo


def _cumsum_rows(tril_bf, x):
    hi, mid, lo = _split3(x)
    return _dot(tril_bf, hi) + _dot(tril_bf, mid) + _dot(tril_bf, lo)


def _cumsum_lanes(x, triu_bf):
    hi, mid, lo = _split3(x)
    return _dot(hi, triu_bf) + _dot(mid, triu_bf) + _dot(lo, triu_bf)


def _tri(n, lower):
    r = lax.broadcasted_iota(jnp.int32, (n, n), 0)
    c = lax.broadcasted_iota(jnp.int32, (n, n), 1)
    return jnp.where((r >= c) if lower else (r <= c), 1.0, 0.0).astype(BF16)


def _log_sigmoid(x):
    return jnp.minimum(x, 0.0) - jnp.log1p(jnp.exp(-jnp.abs(x)))


def _head_mask(width, h):
    lane = lax.broadcasted_iota(jnp.int32, (1, width), 1)
    return (lane // HEAD_DIM) == h


def _head_rmsnorm(x, gh, n_heads):
    width = n_heads * HEAD_DIM
    x2 = x * x
    inv = jnp.zeros_like(x)
    for h in range(n_heads):
        mk = _head_mask(width, h)
        ms = jnp.sum(jnp.where(mk, x2, 0.0), axis=1, keepdims=True) * (1.0 / HEAD_DIM)
        inv = jnp.where(mk, lax.rsqrt(ms + RMS_EPS), inv)
    return x * inv * gh


def _layernorm(x, g, b):
    mu = jnp.mean(x, axis=1, keepdims=True)
    xc = x - mu
    var = jnp.mean(xc * xc, axis=1, keepdims=True)
    return xc * lax.rsqrt(var + LN_EPS) * g + b


def _params(sem):
    return pltpu.CompilerParams(dimension_semantics=sem, vmem_limit_bytes=VMEM_LIMIT)


def _token_tile(n):
    for t in (512, 256, 128, 64, 32, 16, 8):
        if n % t == 0:
            return t
    raise ValueError(f"token count {n} is not a multiple of 8")


def _const_spec(shape):
    nd = len(shape)
    return pl.BlockSpec(shape, lambda *_: (0,) * nd)


def _inproj_kernel(x_ref, w_ref, gb_ref, zm_ref, zf_ref, zcq_ref, zckv_ref, zg_ref, kb_ref, vb_ref):
    xb = _bf(x_ref[...])
    for c in range(4):
        zm_ref[:, c * 256:(c + 1) * 256] = _dot(xb, w_ref[:, C_M + c * 256:C_M + (c + 1) * 256])
    for c in range(3):
        z = _dot(xb, w_ref[:, C_F + c * 256:C_F + (c + 1) * 256])
        zf_ref[:, c * 256:(c + 1) * 256] = z
        if c == 1:
            kb_ref[...] = _bf(z)
        if c == 2:
            vb_ref[...] = _bf(z)
    zcq_ref[...] = _dot(xb, w_ref[:, C_CQ:C_CQ + 256])
    last = _dot(xb, w_ref[:, C_CKV:D_INP])
    zckv_ref[...] = last[:, :KV_LORA]
    g = last[:, KV_LORA:] + gb_ref[...]
    lane = lax.broadcasted_iota(jnp.int32, g.shape, 1)
    zg_ref[...] = jnp.where((lane >= G_MF) & (lane < G_FF + H_FOX), _log_sigmoid(g), g)


def _inproj(x, w_perm, gbias):
    n, d = x.shape
    tm = _token_tile(n)
    row = lambda w: pl.BlockSpec((tm, w), lambda i: (i, 0))
    wf = H_FOX * HEAD_DIM
    widths = (1024, 768, 256, KV_LORA, LANES, wf, wf)
    dtypes = (F32,) * 5 + (BF16,) * 2
    return pl.pallas_call(
        _inproj_kernel,
        grid=(n // tm,),
        in_specs=[row(d), _const_spec(w_perm.shape), _const_spec(gbias.shape)],
        out_specs=[row(w) for w in widths],
        out_shape=[jax.ShapeDtypeStruct((n, w), dt) for w, dt in zip(widths, dtypes)],
        compiler_params=_params(("parallel",)),
        name="inproj",
    )(x, w_perm, gbias)


def _mlstm_kernel(q_ref, k_ref, v_ref, o_ref, zg_ref, gt_ref, c0_ref, n0_ref, m0_ref, gh_ref,
                  hm_ref, c1_ref, n1_ref, m1_ref, c_sc, n_sc, m_sc, *, chunk, n_valid, n_seq):
    ci = pl.program_id(1)

    W = H_MLSTM * HEAD_DIM

    @pl.when(ci == 0)
    def _():
        for s in range(n_seq):
            for h in range(H_MLSTM):
                rows = slice(h * HEAD_DIM, (h + 1) * HEAD_DIM)
                parts = [jnp.zeros((HEAD_DIM, HEAD_DIM), F32)] * H_MLSTM
                parts[h] = c0_ref[s, rows, :]
                c_sc[s, rows, :] = jnp.concatenate(parts, axis=1)
        n_sc[...] = n0_ref[...]
        m_sc[...] = m0_ref[...]

    for s in range(n_seq):
        _mlstm_chunk(q_ref.at[s], k_ref.at[s], v_ref.at[s], o_ref.at[s], zg_ref.at[s], gt_ref.at[s], gh_ref,
                     hm_ref.at[s], c_sc.at[s], n_sc.at[s], m_sc.at[s], chunk, n_valid)

    @pl.when(ci == pl.num_programs(1) - 1)
    def _():
        for s in range(n_seq):
            for h in range(H_MLSTM):
                rows = slice(h * HEAD_DIM, (h + 1) * HEAD_DIM)
                c1_ref[s, rows, :] = c_sc[s, rows, h * HEAD_DIM:(h + 1) * HEAD_DIM]
        n1_ref[...] = n_sc[...]
        m1_ref[...] = m_sc[...]


def _mlstm_chunk(q_ref, k_ref, v_ref, o_ref, zg_ref, gt_ref, gh_ref, hm_ref, c_sc, n_sc, m_sc, L, n_valid):
    W = H_MLSTM * HEAD_DIM
    q = q_ref[...]
    k = k_ref[...] * (HEAD_DIM ** -0.5)
    v = v_ref[...]
    zg = zg_ref[...]
    gt = gt_ref[...]
    valid_c = lax.broadcasted_iota(jnp.int32, (L, 1), 0) < n_valid
    valid_r = lax.broadcasted_iota(jnp.int32, (1, L), 1) < n_valid
    cum_c = _cumsum_rows(_tri(L, True), jnp.where(valid_c, zg, 0.0))
    cum_r = _cumsum_lanes(jnp.where(valid_r, gt, 0.0), _tri(L, False))
    row = lax.broadcasted_iota(jnp.int32, (L, L), 0)
    col = lax.broadcasted_iota(jnp.int32, (L, L), 1)
    tril = row >= col
    qb, kb, vb = _bf(q), _bf(k), _bf(v)
    cb = _bf(c_sc[...])
    n_prev = n_sc[...]

    h_all = jnp.zeros((L, W), F32)
    w_all = jnp.zeros((L, W), F32)
    cdec_all = jnp.zeros((1, W), F32)
    for h in range(H_MLSTM):
        mk = _head_mask(W, h)
        b_c = cum_c[:, G_MF + h:G_MF + h + 1]
        b_r = cum_r[H_MLSTM + h:H_MLSTM + h + 1, :]
        ig_c = jnp.where(valid_c, zg[:, G_MI + h:G_MI + h + 1], NEG_INF)
        ig_r = jnp.where(valid_r, gt[h:h + 1, :], NEG_INF)
        m_prev = m_sc[h:h + 1, 0:1]
        dlog = jnp.where(tril, b_c - b_r + ig_r, NEG_INF)
        inter = b_c + m_prev
        mt = jnp.maximum(inter, jnp.max(dlog, axis=1, keepdims=True))
        dmat = jnp.exp(dlog - mt)
        qh = jnp.where(mk, q, 0.0)
        qhb = _bf(qh)
        s = _dot_nt(qhb, kb) * dmat
        dec = jnp.exp(inter - mt)
        num = _dot(_bf(s), vb) + dec * _dot(qhb, cb)
        den = jnp.sum(s, axis=1, keepdims=True) + dec * jnp.sum(qh * n_prev, axis=1, keepdims=True)
        hh = num / jnp.maximum(jnp.abs(den), jnp.exp(-mt))
        h_all = jnp.where(mk, hh, h_all)
        m_new = mt[L - 1:L, :]
        b_last = b_c[L - 1:L, :]
        w_c = jnp.exp(b_last - b_c + ig_c - m_new)
        cdec = jnp.exp(b_last + m_prev - m_new)
        w_all = jnp.where(mk, w_c, w_all)
        cdec_all = jnp.where(mk, cdec, cdec_all)
        m_sc[h:h + 1, :] = jnp.broadcast_to(m_new, (1, LANES))

    kw = k * w_all
    r2 = lax.broadcasted_iota(jnp.int32, (W, W), 0) // HEAD_DIM
    c2 = lax.broadcasted_iota(jnp.int32, (W, W), 1) // HEAD_DIM
    c_sc[...] = jnp.where(r2 == c2, cdec_all * c_sc[...] + _dot_tn(_bf(kw), vb), 0.0)
    n_sc[...] = cdec_all * n_prev + jnp.sum(kw, axis=0, keepdims=True)

    hm_ref[...] = _head_rmsnorm(h_all, gh_ref[...], H_MLSTM) * jax.nn.sigmoid(o_ref[...])


def _mlstm(zm3, zg3, gt3, c0, n0, m0, gh, n_valid):
    b, t, _ = zm3.shape
    chunk = MLSTM_CHUNK if t % MLSTM_CHUNK == 0 else t
    nc = t // chunk
    w = H_MLSTM * HEAD_DIM
    ns = MLSTM_SEQS if nc == 1 else 1
    while b % ns:
        ns //= 2
    col = lambda j: pl.BlockSpec((ns, chunk, w), lambda bi, ci: (bi, ci, j))
    state = lambda shape: pl.BlockSpec((ns,) + shape, lambda bi, ci: (bi, 0, 0))
    return pl.pallas_call(
        functools.partial(_mlstm_kernel, chunk=chunk, n_valid=n_valid, n_seq=ns),
        grid=(b // ns, nc),
        in_specs=[col(0), col(1), col(2), col(3),
                  pl.BlockSpec((ns, chunk, LANES), lambda bi, ci: (bi, ci, 0)),
                  pl.BlockSpec((ns, G_ROWS, chunk), lambda bi, ci: (bi, 0, ci)),
                  state((w, HEAD_DIM)), state((1, w)), state((SUBLANES, LANES)),
                  _const_spec((1, w))],
        out_specs=[pl.BlockSpec((ns, chunk, w), lambda bi, ci: (bi, ci, 0)),
                   state((w, HEAD_DIM)), state((1, w)), state((SUBLANES, LANES))],
        out_shape=[jax.ShapeDtypeStruct((b, t, w), F32),
                   jax.ShapeDtypeStruct((b, w, HEAD_DIM), F32),
                   jax.ShapeDtypeStruct((b, 1, w), F32),
                   jax.ShapeDtypeStruct((b, SUBLANES, LANES), F32)],
        scratch_shapes=[pltpu.VMEM((ns, w, w), F32), pltpu.VMEM((ns, 1, w), F32),
                        pltpu.VMEM((ns, SUBLANES, LANES), F32)],
        compiler_params=_params(("parallel", "arbitrary")),
        name="mlstm",
    )(zm3, zm3, zm3, zm3, zg3, gt3, c0, n0, m0, gh)


def _fcum_kernel(gt_ref, fr_ref, cr_sc, *, chunk):
    L = chunk

    @pl.when(pl.program_id(1) == 0)
    def _():
        cr_sc[...] = jnp.zeros_like(cr_sc)

    cum_r = _cumsum_lanes(gt_ref[0], _tri(L, False)) + cr_sc[:, 0:1]
    fr_ref[0] = cum_r
    cr_sc[...] = jnp.broadcast_to(cum_r[:, L - 1:L], cr_sc.shape)


def _fcum(gt3):
    b, _, t = gt3.shape
    chunk = 512 if t % 512 == 0 else t
    return pl.pallas_call(
        functools.partial(_fcum_kernel, chunk=chunk),
        grid=(b, t // chunk),
        in_specs=[pl.BlockSpec((1, G_ROWS, chunk), lambda bi, ci: (bi, 0, ci))],
        out_specs=pl.BlockSpec((1, G_ROWS, chunk), lambda bi, ci: (bi, 0, ci)),
        out_shape=jax.ShapeDtypeStruct(gt3.shape, F32),
        scratch_shapes=[pltpu.VMEM((G_ROWS, LANES), F32)],
        compiler_params=_params(("parallel", "arbitrary")),
        name="fcum",
    )(gt3)


def _lane_tile(x, n):
    return x if n == 1 else jnp.concatenate([x] * n, axis=1)


def _staged(n, lag, *stages):
    for i in range(n + lag * (len(stages) - 1)):
        for s, stage in enumerate(stages):
            if 0 <= i - s * lag < n:
                stage(i - s * lag)


def _fox_kernel(q_ref, k_ref, v_ref, fr_ref, gh_ref, o_ref, qh_sc, acc_sc, m_sc, l_sc, *stage_sc, tq, tk):
    s_sc, p_sc, al_sc = stage_sc[0:H_FOX], stage_sc[H_FOX:2 * H_FOX], stage_sc[2 * H_FOX:]
    W = H_FOX * HEAD_DIM
    nt = tk // LANES
    qi = pl.program_id(1)
    n_full = (qi * tq) // tk
    q = q_ref[0]
    for h in range(H_FOX):
        qh_sc[h] = _bf(jnp.where(_head_mask(W, h), q, 0.0))
    acc_sc[...] = jnp.zeros_like(acc_sc)
    m_sc[...] = jnp.full_like(m_sc, NEG_INF)
    l_sc[...] = jnp.zeros_like(l_sc)

    def step(j, masked):
        off = pl.multiple_of(j * tk, tk)
        kb = k_ref[0, pl.ds(off, tk), :]
        vb = v_ref[0, pl.ds(off, tk), :]
        fk = fr_ref[0, :, pl.ds(off, tk)] * LOG2E
        if masked:
            qpos = qi * tq + lax.broadcasted_iota(jnp.int32, (tq, tk), 0)
            kpos = off + lax.broadcasted_iota(jnp.int32, (tq, tk), 1)
            causal = kpos <= qpos

        def scores(h):
            s_sc[h][...] = _dot_nt(qh_sc[h], kb)

        def softmax(h):
            t = s_sc[h][...] * (HEAD_DIM ** -0.5 * LOG2E) - fk[2 * H_MLSTM + h:2 * H_MLSTM + h + 1, :]
            if masked:
                t = jnp.where(causal, t, NEG_INF)
            m_prev = m_sc[h]
            m_new = jnp.maximum(m_prev, jnp.max(t, axis=1, keepdims=True))
            alpha = jnp.exp2(m_prev - m_new)
            p = jnp.exp2(t - _lane_tile(m_new, nt))
            l_sc[h] = alpha * l_sc[h] + jnp.sum(p, axis=1, keepdims=True)
            m_sc[h] = m_new
            al_sc[h][...] = alpha
            p_sc[h][...] = _bf(p)

        def values(h):
            acc_sc[h] = _lane_tile(al_sc[h][...], W // LANES) * acc_sc[h] + _dot(p_sc[h][...], vb)

        _staged(H_FOX, STAGE_LAG, scores, softmax, values)

    def full_step(j, carry):
        step(j, False)
        return carry

    lax.fori_loop(0, n_full, full_step, 0)
    step(n_full, True)

    out = jnp.zeros((tq, W), F32)
    for h in range(H_FOX):
        out = jnp.where(_head_mask(W, h), acc_sc[h] * _lane_tile(1.0 / l_sc[h], W // LANES), out)
    o_ref[0] = _head_rmsnorm(out, gh_ref[...], H_FOX)


def _fox_prompt(zf3, kb3, vb3, fr3, gh):
    b, t, _ = zf3.shape
    w = H_FOX * HEAD_DIM
    tq = ATT_TQ if t % ATT_TQ == 0 else t
    tk = ATT_TK if t % ATT_TK == 0 else t
    seq = lambda shape: pl.BlockSpec((1,) + shape, lambda bi, qi: (bi, 0, 0))
    return pl.pallas_call(
        functools.partial(_fox_kernel, tq=tq, tk=tk),
        grid=(b, t // tq),
        in_specs=[pl.BlockSpec((1, tq, w), lambda bi, qi: (bi, qi, 0)),
                  seq((t, w)), seq((t, w)), seq((G_ROWS, t)), _const_spec((1, w))],
        out_specs=pl.BlockSpec((1, tq, w), lambda bi, qi: (bi, qi, 0)),
        out_shape=jax.ShapeDtypeStruct((b, t, w), F32),
        scratch_shapes=([pltpu.VMEM((H_FOX, tq, w), BF16), pltpu.VMEM((H_FOX, tq, w), F32),
                         pltpu.VMEM((H_FOX, tq, LANES), F32), pltpu.VMEM((H_FOX, tq, LANES), F32)]
                        + [pltpu.VMEM((tq, tk), F32)] * H_FOX + [pltpu.VMEM((tq, tk), BF16)] * H_FOX
                        + [pltpu.VMEM((tq, LANES), F32)] * H_FOX),
        compiler_params=_params(("parallel", "arbitrary")),
        name="fox_prompt",
    )(zf3, kb3, vb3, fr3, gh)


def _rope_swap(x, group):
    w = x.shape[1]
    half = ROPE_DIM // 2
    lane = lax.broadcasted_iota(jnp.int32, x.shape, 1)
    first = (lane % group) < half
    return jnp.where(first, pltpu.roll(x, w - half, 1), pltpu.roll(x, half, 1))


def _mla_pre_kernel(zcq_ref, zckv_ref, zg_ref, cs_ref, sn_ref, gcq_ref, wn_ref, wr_ref, gckv_ref, wuk_ref, sel_ref,
                    q_ref, lat_ref, latp_ref):
    zcq = zcq_ref[...]
    cqn = zcq * lax.rsqrt(jnp.mean(zcq * zcq, axis=1, keepdims=True) + RMS_EPS) * gcq_ref[...]
    cqb = _bf(cqn)
    qn = _dot(cqb, wn_ref[...])
    qr = _dot(cqb, wr_ref[...])
    cs = cs_ref[...]
    sn = sn_ref[...]
    qrot = qr * cs + _rope_swap(qr, ROPE_DIM) * sn
    qlat = _dot(_bf(qn), wuk_ref[...])
    qrope = _dot(_bf(qrot), sel_ref[...])
    for h in range(H_MLA):
        q_ref[0, h, :, 0:LANES] = qlat[:, h * LANES:(h + 1) * LANES].astype(q_ref.dtype)
        q_ref[0, h, :, LANES:2 * LANES] = qrope[:, h * LANES:(h + 1) * LANES].astype(q_ref.dtype)

    zckv = zckv_ref[...]
    ckvn = zckv * lax.rsqrt(jnp.mean(zckv * zckv, axis=1, keepdims=True) + RMS_EPS) * gckv_ref[...]
    zg = zg_ref[...]
    krot = zg * cs[:, :LANES] + _rope_swap(zg, ROPE_DIM) * sn[:, :LANES]
    lane = lax.broadcasted_iota(jnp.int32, krot.shape, 1)
    krz = jnp.where(lane < ROPE_DIM, krot, jnp.where(lane == LANES - 1, 1.0, 0.0))
    lat_ref[:, 0:KV_LORA] = ckvn
    lat_ref[:, KV_LORA:KV_LORA + ROPE_DIM] = krot[:, 0:ROPE_DIM]
    latp_ref[:, 0:KV_LORA] = _bf(ckvn)
    latp_ref[:, KV_LORA:2 * KV_LORA] = _bf(krz)


def _mla_pre(zcq, zckv, zg, cs, sn, lw, bq, tq, q_dtype):
    n = zcq.shape[0]
    tm = _token_tile(min(n, tq))
    tm = min(tm, tq)
    nt_tab = cs.shape[0] // tm
    nt_q = tq // tm
    row = lambda w: pl.BlockSpec((tm, w), lambda i: (i, 0))
    tab = pl.BlockSpec((tm, 2 * LANES), lambda i: (i % nt_tab, 0))
    consts = (lw["g_cq"], lw["w_uq_n"], lw["w_uq_r"], lw["g_ckv"], lw["wuk_bd"], lw["sel"])
    return pl.pallas_call(
        _mla_pre_kernel,
        grid=(n // tm,),
        in_specs=[row(256), row(KV_LORA), row(LANES), tab, tab] + [_const_spec(c.shape) for c in consts],
        out_specs=[pl.BlockSpec((1, H_MLA, tm, 2 * LANES), lambda i: (i // nt_q, 0, i % nt_q, 0)),
                   row(KV_LORA + ROPE_DIM), row(2 * LANES)],
        out_shape=[jax.ShapeDtypeStruct((bq, H_MLA, tq, 2 * LANES), q_dtype),
                   jax.ShapeDtypeStruct((n, KV_LORA + ROPE_DIM), F32),
                   jax.ShapeDtypeStruct((n, 2 * LANES), BF16)],
        compiler_params=_params(("parallel",)),
        name="mla_pre",
    )(zcq, zckv, zg, cs, sn, *consts)


def _mla_out(acc, inv_l, wuv_ref, gh, rows):
    out = jnp.zeros((rows, H_MLA * HEAD_DIM), F32)
    for h in range(H_MLA):
        o_h = acc[h * rows:(h + 1) * rows] * inv_l[h * rows:(h + 1) * rows]
        out = out + _dot(_bf(o_h), wuv_ref[h])
    return _head_rmsnorm(out, gh, H_MLA)


def _mla_kernel(q_ref, k_ref, wuv_ref, gh_ref, o_ref, acc_sc, m_sc, *stage_sc, tq, tk):
    s_sc, p_sc, al_sc = stage_sc[0:H_MLA], stage_sc[H_MLA:2 * H_MLA], stage_sc[2 * H_MLA:]
    nt = tk // LANES
    scale2 = (NOPE_DIM + ROPE_DIM) ** -0.5 * LOG2E
    qi = pl.program_id(1)
    n_full = (qi * tq) // tk
    acc_sc[...] = jnp.zeros_like(acc_sc)
    m_sc[...] = jnp.full_like(m_sc, NEG_INF)

    def step(j, masked):
        off = pl.multiple_of(j * tk, tk)
        kb = k_ref[0, pl.ds(off, tk), :]
        if masked:
            qpos = qi * tq + lax.broadcasted_iota(jnp.int32, (tq, tk), 0)
            kpos = off + lax.broadcasted_iota(jnp.int32, (tq, tk), 1)
            causal = kpos <= qpos
        def scores(h):
            s_sc[h][...] = _dot_nt(q_ref[0, h], kb)

        def softmax(h):
            s = s_sc[h][...]
            if masked:
                s = jnp.where(causal, s, NEG_INF)
            m_prev = m_sc[h]
            m_new = jnp.maximum(m_prev, jnp.max(s, axis=1, keepdims=True))
            al_sc[h][...] = jnp.exp2((m_prev - m_new) * scale2)
            p_sc[h][...] = _bf(jnp.exp2((s - _lane_tile(m_new, nt)) * scale2))
            m_sc[h] = m_new

        def values(h):
            acc_sc[h] = _lane_tile(al_sc[h][...], 2) * acc_sc[h] + _dot(p_sc[h][...], kb)

        _staged(H_MLA, STAGE_LAG, scores, softmax, values)

    def full_step(j, carry):
        step(j, False)
        return carry

    lax.fori_loop(0, n_full, full_step, 0)
    step(n_full, True)

    out = jnp.zeros((tq, H_MLA * HEAD_DIM), F32)
    for h in range(H_MLA):
        acc = acc_sc[h]
        o_h = acc[:, :KV_LORA] * (1.0 / acc[:, 2 * LANES - 1:2 * LANES])
        out = out + _dot(_bf(o_h), wuv_ref[h])
    o_ref[0] = _head_rmsnorm(out, gh_ref[...], H_MLA)


def _mla_prompt(qp, latp3, wuv_pad, gh):
    b, _, t, _ = qp.shape
    tq = ATT_TQ if t % ATT_TQ == 0 else t
    tk = ATT_TK if t % ATT_TK == 0 else t
    w = H_MLA * HEAD_DIM
    return pl.pallas_call(
        functools.partial(_mla_kernel, tq=tq, tk=tk),
        grid=(b, t // tq),
        in_specs=[pl.BlockSpec((1, H_MLA, tq, 2 * LANES), lambda bi, qi: (bi, 0, qi, 0)),
                  pl.BlockSpec((1, t, 2 * LANES), lambda bi, qi: (bi, 0, 0)),
                  _const_spec(wuv_pad.shape), _const_spec((1, w))],
        out_specs=pl.BlockSpec((1, tq, w), lambda bi, qi: (bi, qi, 0)),
        out_shape=jax.ShapeDtypeStruct((b, t, w), F32),
        scratch_shapes=([pltpu.VMEM((H_MLA, tq, 2 * LANES), F32), pltpu.VMEM((H_MLA, tq, LANES), F32)]
                        + [pltpu.VMEM((tq, tk), F32)] * H_MLA + [pltpu.VMEM((tq, tk), BF16)] * H_MLA
                        + [pltpu.VMEM((tq, LANES), F32)] * H_MLA),
        compiler_params=_params(("parallel", "arbitrary")),
        name="mla_prompt",
    )(qp, latp3, wuv_pad, gh)


def _decode_kernel(pt_ref, kt_hbm, vt_hbm, lt_hbm, lf_hbm, kn_ref, vn_ref, ln_ref, gtn_ref, qf_ref, qa_ref,
                   wuv_ref, ghf_ref, gha_ref, of_ref, oa_ref,
                   k_buf, v_buf, l_buf, f_buf, sems, lat16, facc, fm, fl, aacc, am, al, carry,
                   *, n_group, layer, n_new):
    G = n_group
    bi = pl.program_id(0)
    j = pl.program_id(1)
    nj = pl.num_programs(1)
    step = bi * nj + j
    last_step = pl.num_programs(0) * nj - 1
    WF = H_FOX * HEAD_DIM
    RF = H_FOX * PAD_T
    RA = H_MLA * PAD_T

    def page_copies(s, slot):
        b_s = s // nj
        first = (s % nj) * 2 * G + slot * G
        out = []
        for g in range(G):
            page = pt_ref[b_s, first + g]
            out.append(pltpu.make_async_copy(kt_hbm.at[layer, page], k_buf.at[slot, g], sems.at[slot, 0]))
            out.append(pltpu.make_async_copy(vt_hbm.at[layer, page], v_buf.at[slot, g], sems.at[slot, 1]))
            out.append(pltpu.make_async_copy(lt_hbm.at[layer, page], l_buf.at[slot, g], sems.at[slot, 2]))
            out.append(pltpu.make_async_copy(lf_hbm.at[layer, page],
                                             f_buf.at[slot, pl.ds(g * SUBLANES, H_FOX), :], sems.at[slot, 3]))
        return out

    @pl.when(step == 0)
    def _():
        for slot in range(2):
            for g in range(G):
                f_buf[slot, g * SUBLANES + H_FOX:(g + 1) * SUBLANES, :] = jnp.zeros((SUBLANES - H_FOX, LANES), F32)
        lat16[...] = jnp.zeros_like(lat16)
        for slot in range(2):
            for c in page_copies(step, slot):
                c.start()

    @pl.when(j == 0)
    def _():
        facc[...] = jnp.zeros_like(facc)
        fm[...] = jnp.full_like(fm, NEG_INF)
        fl[...] = jnp.zeros_like(fl)
        aacc[...] = jnp.zeros_like(aacc)
        am[...] = jnp.full_like(am, NEG_INF)
        al[...] = jnp.zeros_like(al)
        carry[...] = jnp.zeros_like(carry)

    q = qf_ref[0]
    qbd = jnp.concatenate([_bf(jnp.where(_head_mask(WF, h), q, 0.0)) for h in range(H_FOX)], axis=0)
    qa = _bf(qa_ref[0].reshape(RA, 2 * LANES))
    triu = _tri(LANES, False)
    scale_f = HEAD_DIM ** -0.5 * LOG2E
    scale_a = (NOPE_DIM + ROPE_DIM) ** -0.5 * LOG2E

    def like_keys(m, width):
        return _lane_tile(m, width // LANES) if width % LANES == 0 else m[:, :width]

    def merge(s_f, s_a, pv_f, pv_a):
        width = s_f.shape[1]
        m_prev = fm[...]
        m_new = jnp.maximum(m_prev, jnp.max(s_f, axis=1, keepdims=True))
        alpha = jnp.exp2(m_prev - m_new)
        p = jnp.exp2(s_f - like_keys(m_new, width))
        fl[...] = alpha * fl[...] + jnp.sum(p, axis=1, keepdims=True)
        fm[...] = m_new
        facc[...] = _lane_tile(alpha, WF // LANES) * facc[...] + pv_f(_bf(p))
        m_prev = am[...]
        m_new = jnp.maximum(m_prev, jnp.max(s_a, axis=1, keepdims=True))
        alpha = jnp.exp2((m_prev - m_new) * scale_a)
        p = jnp.exp2((s_a - like_keys(m_new, width)) * scale_a)
        al[...] = alpha * al[...] + jnp.sum(p, axis=1, keepdims=True)
        am[...] = m_new
        aacc[...] = alpha * aacc[...] + pv_a(_bf(p))

    def head_rows(cum, g0):
        return jnp.concatenate([jnp.broadcast_to(cum[g0 + h:g0 + h + 1, :], (PAD_T, cum.shape[1]))
                                for h in range(H_FOX)], axis=0)

    def block(kts, vts, lts, lf8, kbuf):
        n = len(kts)
        cs = _cumsum_lanes(lf8, triu)
        tot = jnp.broadcast_to(cs[:, LANES - 1:LANES], cs.shape)
        base = jnp.concatenate([carry[...]] * n, axis=0)
        if n > 1:
            r = lax.broadcasted_iota(jnp.int32, (n * SUBLANES, n * SUBLANES), 0)
            c = lax.broadcasted_iota(jnp.int32, (n * SUBLANES, n * SUBLANES), 1)
            earlier = jnp.where((r % SUBLANES == c % SUBLANES) & (c // SUBLANES < r // SUBLANES), 1.0, 0.0)
            base = base + _cumsum_rows(earlier.astype(BF16), tot)
        last = slice((n - 1) * SUBLANES, n * SUBLANES)
        carry[...] = base[last] + tot[last]
        cum = (cs + base) * LOG2E
        s_f, s_a = [], []
        for g in range(n):
            s_f.append(_dot(qbd, _bf(kts[g][...])) * scale_f - head_rows(cum, g * SUBLANES))
            kbuf[g, 0:KV_LORA + ROPE_DIM, :] = _bf(lts[g][...])
            s_a.append(_dot(qa, kbuf[g]))

        def pv_f(p):
            return sum(_dot_nt(p[:, g * LANES:(g + 1) * LANES], _bf(vts[g][...])) for g in range(n))

        def pv_a(p):
            return sum(_dot_nt(p[:, g * LANES:(g + 1) * LANES], kbuf[g, 0:KV_LORA, :]) for g in range(n))

        merge(jnp.concatenate(s_f, axis=1), jnp.concatenate(s_a, axis=1), pv_f, pv_a)

    def new_token_block():
        kn, vn, ln = _bf(kn_ref[0]), _bf(vn_ref[0]), ln_ref[0]
        valid = lax.broadcasted_iota(jnp.int32, (1, PAD_T), 1) < n_new
        cs = _cumsum_lanes(jnp.where(valid, gtn_ref[0], 0.0), _tri(PAD_T, False))
        cum = (cs[2 * H_MLSTM:2 * H_MLSTM + H_FOX] + carry[0:H_FOX, 0:PAD_T]) * LOG2E
        s_f = _dot_nt(qbd, kn) * scale_f - head_rows(cum, 0)
        s_a = _dot_nt(qa, ln)
        ok_f = lax.broadcasted_iota(jnp.int32, s_f.shape, 1) <= lax.broadcasted_iota(jnp.int32, s_f.shape, 0) % PAD_T
        ok_a = lax.broadcasted_iota(jnp.int32, s_a.shape, 1) <= lax.broadcasted_iota(jnp.int32, s_a.shape, 0) % PAD_T
        merge(jnp.where(ok_f, s_f, NEG_INF), jnp.where(ok_a, s_a, NEG_INF),
              lambda p: _dot(p, vn), lambda p: _dot(p, ln[:, :KV_LORA]))

    def half_step(slot):
        for c in page_copies(step, slot):
            c.wait()
        block([k_buf.at[slot, g] for g in range(G)], [v_buf.at[slot, g] for g in range(G)],
              [l_buf.at[slot, g] for g in range(G)], f_buf[slot], lat16.at[slot])

    nxt = jnp.minimum(step + 1, last_step)
    half_step(0)
    for c in page_copies(nxt, 0):
        c.start()
    half_step(1)
    for c in page_copies(nxt, 1):
        c.start()

    @pl.when(step == last_step)
    def _():
        for slot in range(2):
            for c in page_copies(step, slot):
                c.wait()

    @pl.when(j == nj - 1)
    def _():
        new_token_block()
        accf = facc[...] * _lane_tile(1.0 / fl[...], WF // LANES)
        of = jnp.zeros((PAD_T, WF), F32)
        for h in range(H_FOX):
            of = jnp.where(_head_mask(WF, h), accf[h * PAD_T:(h + 1) * PAD_T], of)
        of_ref[0] = _head_rmsnorm(of, ghf_ref[...], H_FOX)
        oa_ref[0] = _mla_out(aacc[...], 1.0 / al[...], wuv_ref, gha_ref[...], PAD_T)


def _decode(page_table, layer, caches, zf3, latp3, gt3, qa, wuv_pad, ghf, gha, n_new):
    b, n_pages = page_table.shape
    assert n_pages % 2 == 0, "the two-slot page pipeline needs an even number of pages per sequence"
    G = DECODE_PAGES
    while n_pages % (2 * G):
        G //= 2
    kt, vt, lt, lf = caches
    page = kt.shape[-1]

    def seq_spec(arr):
        nd = arr.ndim
        return pl.BlockSpec((1,) + arr.shape[1:], lambda bi, ji, pt: (bi,) + (0,) * (nd - 1))

    def const(arr):
        nd = arr.ndim
        return pl.BlockSpec(arr.shape, lambda bi, ji, pt: (0,) * nd)

    wf = H_FOX * HEAD_DIM
    wa = H_MLA * HEAD_DIM
    zcol = lambda c: pl.BlockSpec((1, PAD_T, wf), lambda bi, ji, pt: (bi, 0, c))
    in_specs = [pl.BlockSpec(memory_space=pl.ANY)] * 4 + [zcol(1), zcol(2), seq_spec(latp3), seq_spec(gt3),
                                                           zcol(0), seq_spec(qa)]
    operands = [kt, vt, lt, lf, zf3, zf3, latp3, gt3, zf3, qa]
    for arr in (wuv_pad, ghf, gha):
        in_specs.append(const(arr))
        operands.append(arr)
    grid_spec = pltpu.PrefetchScalarGridSpec(
        num_scalar_prefetch=1,
        grid=(b, n_pages // (2 * G)),
        in_specs=in_specs,
        out_specs=[pl.BlockSpec((1, PAD_T, wf), lambda bi, ji, pt: (bi, 0, 0)),
                   pl.BlockSpec((1, PAD_T, wa), lambda bi, ji, pt: (bi, 0, 0))],
        scratch_shapes=[pltpu.VMEM((2, G) + kt.shape[2:], F32), pltpu.VMEM((2, G) + vt.shape[2:], F32),
                        pltpu.VMEM((2, G) + lt.shape[2:], F32), pltpu.VMEM((2, G * SUBLANES, page), F32),
                        pltpu.SemaphoreType.DMA((2, 4)),
                        pltpu.VMEM((2, G, 2 * LANES, page), BF16),
                        pltpu.VMEM((H_FOX * PAD_T, wf), F32), pltpu.VMEM((H_FOX * PAD_T, LANES), F32),
                        pltpu.VMEM((H_FOX * PAD_T, LANES), F32),
                        pltpu.VMEM((H_MLA * PAD_T, KV_LORA), F32), pltpu.VMEM((H_MLA * PAD_T, LANES), F32),
                        pltpu.VMEM((H_MLA * PAD_T, LANES), F32),
                        pltpu.VMEM((SUBLANES, LANES), F32)])
    return pl.pallas_call(
        functools.partial(_decode_kernel, n_group=G, layer=layer, n_new=n_new),
        grid_spec=grid_spec,
        out_shape=[jax.ShapeDtypeStruct((b, PAD_T, wf), F32), jax.ShapeDtypeStruct((b, PAD_T, wa), F32)],
        compiler_params=_params(("arbitrary", "arbitrary")),
        name="decode",
    )(page_table, *operands)


def _out_ffn_kernel(x_ref, hm_ref, hf_ref, ha_ref, wom_ref, wof_ref, woa_ref, g1_ref, b1_ref,
                    wg_ref, wu_ref, wd_ref, g2_ref, b2_ref, y_ref, acc_sc, *, alpha, n_chunks):
    mix = (_dot(_bf(hm_ref[...]), wom_ref[...]) + _dot(_bf(hf_ref[...]), wof_ref[...])
           + _dot(_bf(ha_ref[...]), woa_ref[...]))
    x1 = _layernorm(alpha * x_ref[...] + mix, g1_ref[...], b1_ref[...])
    x1b = _bf(x1)
    acc_sc[...] = jnp.zeros_like(acc_sc)

    for c in range(n_chunks):
        cols = slice(c * FF_CHUNK, (c + 1) * FF_CHUNK)
        g = _dot(x1b, wg_ref[:, cols])
        u = _dot(x1b, wu_ref[:, cols])
        acc_sc[...] += _dot(_bf(g * jax.nn.sigmoid(g) * u), wd_ref[cols, :])
    y_ref[...] = _layernorm(alpha * x1 + acc_sc[...], g2_ref[...], b2_ref[...])


def _out_ffn(x, hm, hf, ha, lw, alpha):
    n, d = x.shape
    tm = _token_tile(n)
    row = lambda w: pl.BlockSpec((tm, w), lambda i: (i, 0))
    consts = (lw["wo_m"], lw["wo_f"], lw["wo_a"], lw["ln1_g"], lw["ln1_b"],
              lw["wg"], lw["wu"], lw["wd"], lw["ln2_g"], lw["ln2_b"])

    def resident(c):
        nd = c.ndim
        return pl.BlockSpec(c.shape, lambda i: (0,) * nd, pipeline_mode=pl.Buffered(1))

    return pl.pallas_call(
        functools.partial(_out_ffn_kernel, alpha=alpha, n_chunks=lw["wg"].shape[1] // FF_CHUNK),
        grid=(n // tm,),
        in_specs=[row(d), row(hm.shape[1]), row(hf.shape[1]), row(ha.shape[1])] + [resident(c) for c in consts],
        out_specs=row(d),
        out_shape=jax.ShapeDtypeStruct((n, d), F32),
        scratch_shapes=[pltpu.VMEM((tm, d), F32)],
        compiler_params=_params(("parallel",)),
        name="out_ffn",
    )(x, hm, hf, ha, *consts)


FF_CHUNK = 256


def _prep_layer(l, w_in, b_mi, b_mf, b_ff, g_cq, w_uq, g_ckv, w_uk, w_uv, g_head, w_out,
                ln1_g, ln1_b, w_gate, w_up, w_down, ln2_g, ln2_b):
    d = w_in.shape[1]
    wm = H_MLSTM * HEAD_DIM
    wf = H_FOX * HEAD_DIM
    q_lora = g_cq.shape[1]
    split = (wm,) * 4 + (H_MLSTM, H_MLSTM) + (wf,) * 3 + (H_FOX,) + (q_lora, KV_LORA, ROPE_DIM)
    at = [int(i) for i in np.cumsum(split)[:-1]]
    mq, mk, mv, mo, mi, mf, fq, fk, fv, ff, cq, ckv, kr = jnp.split(w_in[l], at, axis=1)
    pad = jnp.zeros((d, LANES - ROPE_DIM - 2 * H_MLSTM - H_FOX), F32)
    w_perm = _bf(jnp.concatenate([mq, mk, mv, mo, fq, fk, fv, cq, ckv, kr, mi, mf, ff, pad], axis=1))
    gbias = jnp.zeros((1, LANES), F32)
    gbias = gbias.at[0, G_MI:G_MI + H_MLSTM].set(b_mi[l]).at[0, G_MF:G_MF + H_MLSTM].set(b_mf[l])
    gbias = gbias.at[0, G_FF:G_FF + H_FOX].set(b_ff[l])
    uq = w_uq[l].reshape(q_lora, H_MLA, NOPE_DIM + ROPE_DIM)
    eye = jnp.eye(H_MLA, dtype=F32)
    sel = np.zeros((H_MLA * ROPE_DIM, H_MLA * LANES), np.float32)
    for h in range(H_MLA):
        sel[h * ROPE_DIM + np.arange(ROPE_DIM), h * LANES + np.arange(ROPE_DIM)] = 1.0
    assert w_gate.shape[2] % FF_CHUNK == 0
    return dict(
        w_perm=w_perm, gbias=gbias,
        g_cq=g_cq[l][None], g_ckv=g_ckv[l][None],
        w_uq_n=_bf(uq[:, :, :NOPE_DIM].reshape(q_lora, H_MLA * NOPE_DIM)),
        w_uq_r=_bf(uq[:, :, NOPE_DIM:].reshape(q_lora, H_MLA * ROPE_DIM)),
        wuk_bd=_bf(jnp.einsum("chd,hg->hdgc", w_uk[l], eye).reshape(H_MLA * NOPE_DIM, H_MLA * KV_LORA)),
        sel=jnp.asarray(sel, BF16),
        wuv_pad=_bf(jnp.einsum("chd,hg->hcgd", w_uv[l], eye).reshape(H_MLA, KV_LORA, H_MLA * HEAD_DIM)),
        gh_m=g_head[l][None, :wm], gh_f=g_head[l][None, wm:wm + wf], gh_a=g_head[l][None, wm + wf:],
        wo_m=_bf(w_out[l][:wm]), wo_f=_bf(w_out[l][wm:wm + wf]), wo_a=_bf(w_out[l][wm + wf:]),
        ln1_g=ln1_g[l][None], ln1_b=ln1_b[l][None], ln2_g=ln2_g[l][None], ln2_b=ln2_b[l][None],
        wg=_bf(w_gate[l]), wu=_bf(w_up[l]), wd=_bf(w_down[l]),
    )


def _rope_tables(pos):
    half = ROPE_DIM // 2
    freq = ROPE_THETA ** (-jnp.arange(half, dtype=F32) * (2.0 / ROPE_DIM))
    ang = pos.astype(F32)[:, None] * freq[None, :]
    cos, sin = jnp.cos(ang), jnp.sin(ang)
    reps = 2 * LANES // ROPE_DIM
    return (jnp.tile(jnp.concatenate([cos, cos], axis=1), (1, reps)),
            jnp.tile(jnp.concatenate([-sin, sin], axis=1), (1, reps)))


def _layer_common(x3, lw, cs, sn, c0, n0, m0, n_valid, mla_bq, mla_tq, q_dtype):
    b, t, d = x3.shape
    zm, zf, zcq, zckv, zg, kb, vb = _inproj(x3.reshape(b * t, d), lw["w_perm"], lw["gbias"])
    zg3 = zg.reshape(b, t, LANES)
    gt3 = jnp.swapaxes(zg3[:, :, G_MI:G_MI + G_ROWS], 1, 2)
    hm, c1, n1, m1 = _mlstm(zm.reshape(b, t, -1), zg3, gt3, c0, n0, m0, lw["gh_m"], n_valid)
    qp, lat, latp = _mla_pre(zcq, zckv, zg, cs, sn, lw, mla_bq, mla_tq, q_dtype)
    return zf, (kb, vb), zg3, gt3, hm, (c1, n1, m1), qp, lat, latp


def _state_out(c1, n1, m1):
    b = c1.shape[0]
    return (c1.reshape(b, H_MLSTM, HEAD_DIM, HEAD_DIM), n1.reshape(b, H_MLSTM, HEAD_DIM), m1[:, :H_MLSTM, 0])


def kernel(x_prompt, x_sample, cache_fox_k, cache_fox_v, cache_fox_logf, cache_mla_latent, state_mlstm_C, state_mlstm_n, state_mlstm_m, page_table, w_in, b_mlstm_i, b_mlstm_f, b_fox_f, g_cq, w_uq, g_ckv, w_uk, w_uv, g_head, w_out, ln1_g, ln1_b, w_gate, w_up, w_down, ln2_g, ln2_b):
    depth = w_in.shape[0]
    bp, tp, d = x_prompt.shape
    bs, ts, _ = x_sample.shape
    n_pages = page_table.shape[1]
    page = cache_fox_k.shape[2]
    past = n_pages * page
    alpha = (2 * depth) ** 0.25
    wm = H_MLSTM * HEAD_DIM
    wf = H_FOX * HEAD_DIM

    pools = (jnp.transpose(cache_fox_k, (0, 1, 3, 4, 2)).reshape(depth, -1, wf, page),
             jnp.transpose(cache_fox_v, (0, 1, 3, 4, 2)).reshape(depth, -1, wf, page),
             jnp.transpose(cache_mla_latent, (0, 1, 3, 2)),
             jnp.transpose(cache_fox_logf, (0, 1, 3, 2)))

    cs_p, sn_p = _rope_tables(jnp.arange(tp, dtype=jnp.int32))
    cs_s, sn_s = _rope_tables(past + jnp.arange(PAD_T, dtype=jnp.int32))
    cs_s, sn_s = jnp.tile(cs_s, (bs, 1)), jnp.tile(sn_s, (bs, 1))

    yp = x_prompt
    ys = jnp.pad(x_sample, ((0, 0), (0, PAD_T - ts), (0, 0)))
    outs_p, outs_s = [], []
    for l in range(depth):
        lw = _prep_layer(l, w_in, b_mlstm_i, b_mlstm_f, b_fox_f, g_cq, w_uq, g_ckv, w_uk, w_uv, g_head, w_out,
                         ln1_g, ln1_b, w_gate, w_up, w_down, ln2_g, ln2_b)

        c0 = jnp.zeros((bp, wm, HEAD_DIM), F32)
        n0 = jnp.zeros((bp, 1, wm), F32)
        m0 = jnp.zeros((bp, SUBLANES, LANES), F32)
        zf, (kb, vb), zg3, gt3, hm, st, qp, lat, latp = _layer_common(yp, lw, cs_p, sn_p, c0, n0, m0, tp, bp, tp, BF16)
        fr3 = _fcum(gt3)
        zf3 = zf.reshape(bp, tp, -1)
        hf = _fox_prompt(zf3, kb.reshape(bp, tp, -1), vb.reshape(bp, tp, -1), fr3, lw["gh_f"])
        ha = _mla_prompt(qp, latp.reshape(bp, tp, -1), lw["wuv_pad"], lw["gh_a"])
        yp = _out_ffn(yp.reshape(bp * tp, d), hm.reshape(bp * tp, -1), hf.reshape(bp * tp, -1),
                      ha.reshape(bp * tp, -1), lw, alpha).reshape(bp, tp, d)
        outs_p.append((zf3[:, :, wf:2 * wf].reshape(bp, tp, H_FOX, HEAD_DIM),
                       zf3[:, :, 2 * wf:].reshape(bp, tp, H_FOX, HEAD_DIM),
                       zg3[:, :, G_FF:G_FF + H_FOX],
                       lat.reshape(bp, tp, -1)) + _state_out(*st))

        c0 = state_mlstm_C[l].reshape(bs, wm, HEAD_DIM)
        n0 = state_mlstm_n[l].reshape(bs, 1, wm)
        m0 = jnp.broadcast_to(jnp.pad(state_mlstm_m[l], ((0, 0), (0, SUBLANES - H_MLSTM)))[:, :, None],
                              (bs, SUBLANES, LANES))
        zf, _, zg3, gt3, hm, st, qa, lat, latp = _layer_common(ys, lw, cs_s, sn_s, c0, n0, m0, ts, 1, bs * PAD_T, F32)
        zf3 = zf.reshape(bs, PAD_T, -1)
        qa4 = jnp.swapaxes(qa.reshape(H_MLA, bs, PAD_T, 2 * LANES), 0, 1)
        hf, ha = _decode(page_table, l, pools, zf3, latp.reshape(bs, PAD_T, -1), gt3, qa4,
                         lw["wuv_pad"], lw["gh_f"], lw["gh_a"], ts)
        ys = _out_ffn(ys.reshape(bs * PAD_T, d), hm.reshape(bs * PAD_T, -1), hf.reshape(bs * PAD_T, -1),
                      ha.reshape(bs * PAD_T, -1), lw, alpha).reshape(bs, PAD_T, d)
        outs_s.append((zf3[:, :ts, wf:2 * wf].reshape(bs, ts, H_FOX, HEAD_DIM),
                       zf3[:, :ts, 2 * wf:].reshape(bs, ts, H_FOX, HEAD_DIM),
                       zg3[:, :ts, G_FF:G_FF + H_FOX],
                       lat.reshape(bs, PAD_T, -1)[:, :ts]) + _state_out(*st))

    sp = [jnp.stack(a) for a in zip(*outs_p)]
    ss = [jnp.stack(a) for a in zip(*outs_s)]
    return (yp, ys[:, :ts], sp[0], sp[1], sp[2], sp[3], sp[4], sp[5], sp[6],
            ss[0], ss[1], ss[2], ss[3], ss[4], ss[5], ss[6])
```

```python
import functools

import jax
import jax.numpy as jnp
import numpy as np
from jax import lax
from jax.experimental import pallas as pl
from jax.experimental.pallas import tpu as pltpu

F32 = jnp.float32
BF16 = jnp.bfloat16

HEAD_DIM = 64
H_MLSTM = 4
H_FOX = 4
H_MLA = 8
MLSTM_CHUNK = 256
KV_LORA = 128
NOPE_DIM = 64
ROPE_DIM = 32
ROPE_THETA = 10000.0
LN_EPS = 1e-5
RMS_EPS = 1e-6

LANES = 128
SUBLANES = 8
VMEM_LIMIT = 56 * 1024 * 1024

C_M = 0
C_F = 1024
C_CQ = 1792
C_CKV = 2048
C_G = 2176
D_INP = 2304
G_KR = 0
G_MI = 32
G_MF = 36
G_FF = 40
G_ROWS = 16

PAD_T = 8
NEG_INF = float("-inf")
LOG2E = 1.4426950408889634
STAGE_LAG = 1
ATT_TQ = 512
ATT_TK = 512
DECODE_PAGES = 32
MLSTM_SEQS = 8


def _bf(x):
    return x.astype(BF16)


def _dot(a, b):
    return jnp.dot(a, b, preferred_element_type=F32)


def _dot_nt(a, b):
    return lax.dot_general(a, b, (((1,), (1,)), ((), ())), preferred_element_type=F32)


def _dot_tn(a, b):
    return lax.dot_general(a, b, (((0,), (0,)), ((), ())), preferred_element_type=F32)


def _split3(x):
    hi = _bf(x)
    r = x - hi.astype(F32)
    mid = _bf(r)
    lo = _bf(r - mid.astype(F32))
    return hi, mid, lo


def _cumsum_rows(tril_bf, x):
    hi, mid, lo = _split3(x)
    return _dot(tril_bf, hi) + _dot(tril_bf, mid) + _dot(tril_bf, lo)


def _cumsum_lanes(x, triu_bf):
    hi, mid, lo = _split3(x)
    return _dot(hi, triu_bf) + _dot(mid, triu_bf) + _dot(lo, triu_bf)


def _tri(n, lower):
    r = lax.broadcasted_iota(jnp.int32, (n, n), 0)
    c = lax.broadcasted_iota(jnp.int32, (n, n), 1)
    return jnp.where((r >= c) if lower else (r <= c), 1.0, 0.0).astype(BF16)


def _log_sigmoid(x):
    return jnp.minimum(x, 0.0) - jnp.log1p(jnp.exp(-jnp.abs(x)))


def _head_mask(width, h):
    lane = lax.broadcasted_iota(jnp.int32, (1, width), 1)
    return (lane // HEAD_DIM) == h


def _head_rmsnorm(x, gh, n_heads):
    width = n_heads * HEAD_DIM
    x2 = x * x
    inv = jnp.zeros_like(x)
    for h in range(n_heads):
        mk = _head_mask(width, h)
        ms = jnp.sum(jnp.where(mk, x2, 0.0), axis=1, keepdims=True) * (1.0 / HEAD_DIM)
        inv = jnp.where(mk, lax.rsqrt(ms + RMS_EPS), inv)
    return x * inv * gh


def _layernorm(x, g, b):
    mu = jnp.mean(x, axis=1, keepdims=True)
    xc = x - mu
    var = jnp.mean(xc * xc, axis=1, keepdims=True)
    return xc * lax.rsqrt(var + LN_EPS) * g + b


def _params(sem):
    return pltpu.CompilerParams(dimension_semantics=sem, vmem_limit_bytes=VMEM_LIMIT)


def _token_tile(n):
    for t in (512, 256, 128, 64, 32, 16, 8):
        if n % t == 0:
            return t
    raise ValueError(f"token count {n} is not a multiple of 8")


def _const_spec(shape):
    nd = len(shape)
    return pl.BlockSpec(shape, lambda *_: (0,) * nd)


def _inproj_kernel(x_ref, w_ref, gb_ref, zm_ref, zf_ref, zcq_ref, zckv_ref, zg_ref, kb_ref, vb_ref):
    xb = _bf(x_ref[...])
    for c in range(4):
        zm_ref[:, c * 256:(c + 1) * 256] = _dot(xb, w_ref[:, C_M + c * 256:C_M + (c + 1) * 256])
    for c in range(3):
        z = _dot(xb, w_ref[:, C_F + c * 256:C_F + (c + 1) * 256])
        zf_ref[:, c * 256:(c + 1) * 256] = z
        if c == 1:
            kb_ref[...] = _bf(z)
        if c == 2:
            vb_ref[...] = _bf(z)
    zcq_ref[...] = _dot(xb, w_ref[:, C_CQ:C_CQ + 256])
    last = _dot(xb, w_ref[:, C_CKV:D_INP])
    zckv_ref[...] = last[:, :KV_LORA]
    g = last[:, KV_LORA:] + gb_ref[...]
    lane = lax.broadcasted_iota(jnp.int32, g.shape, 1)
    zg_ref[...] = jnp.where((lane >= G_MF) & (lane < G_FF + H_FOX), _log_sigmoid(g), g)


def _inproj(x, w_perm, gbias):
    n, d = x.shape
    tm = _token_tile(n)
    row = lambda w: pl.BlockSpec((tm, w), lambda i: (i, 0))
    wf = H_FOX * HEAD_DIM
    widths = (1024, 768, 256, KV_LORA, LANES, wf, wf)
    dtypes = (F32,) * 5 + (BF16,) * 2
    return pl.pallas_call(
        _inproj_kernel,
        grid=(n // tm,),
        in_specs=[row(d), _const_spec(w_perm.shape), _const_spec(gbias.shape)],
        out_specs=[row(w) for w in widths],
        out_shape=[jax.ShapeDtypeStruct((n, w), dt) for w, dt in zip(widths, dtypes)],
        compiler_params=_params(("parallel",)),
        name="inproj",
    )(x, w_perm, gbias)


def _mlstm_kernel(q_ref, k_ref, v_ref, o_ref, zg_ref, gt_ref, c0_ref, n0_ref, m0_ref, gh_ref,
                  hm_ref, c1_ref, n1_ref, m1_ref, c_sc, n_sc, m_sc, *, chunk, n_valid, n_seq):
    ci = pl.program_id(1)

    W = H_MLSTM * HEAD_DIM

    @pl.when(ci == 0)
    def _():
        for s in range(n_seq):
            for h in range(H_MLSTM):
                rows = slice(h * HEAD_DIM, (h + 1) * HEAD_DIM)
                parts = [jnp.zeros((HEAD_DIM, HEAD_DIM), F32)] * H_MLSTM
                parts[h] = c0_ref[s, rows, :]
                c_sc[s, rows, :] = jnp.concatenate(parts, axis=1)
        n_sc[...] = n0_ref[...]
        m_sc[...] = m0_ref[...]

    for s in range(n_seq):
        _mlstm_chunk(q_ref.at[s], k_ref.at[s], v_ref.at[s], o_ref.at[s], zg_ref.at[s], gt_ref.at[s], gh_ref,
                     hm_ref.at[s], c_sc.at[s], n_sc.at[s], m_sc.at[s], chunk, n_valid)

    @pl.when(ci == pl.num_programs(1) - 1)
    def _():
        for s in range(n_seq):
            for h in range(H_MLSTM):
                rows = slice(h * HEAD_DIM, (h + 1) * HEAD_DIM)
                c1_ref[s, rows, :] = c_sc[s, rows, h * HEAD_DIM:(h + 1) * HEAD_DIM]
        n1_ref[...] = n_sc[...]
        m1_ref[...] = m_sc[...]


def _mlstm_chunk(q_ref, k_ref, v_ref, o_ref, zg_ref, gt_ref, gh_ref, hm_ref, c_sc, n_sc, m_sc, L, n_valid):
    W = H_MLSTM * HEAD_DIM
    q = q_ref[...]
    k = k_ref[...] * (HEAD_DIM ** -0.5)
    v = v_ref[...]
    zg = zg_ref[...]
    gt = gt_ref[...]
    valid_c = lax.broadcasted_iota(jnp.int32, (L, 1), 0) < n_valid
    valid_r = lax.broadcasted_iota(jnp.int32, (1, L), 1) < n_valid
    cum_c = _cumsum_rows(_tri(L, True), jnp.where(valid_c, zg, 0.0))
    cum_r = _cumsum_lanes(jnp.where(valid_r, gt, 0.0), _tri(L, False))
    row = lax.broadcasted_iota(jnp.int32, (L, L), 0)
    col = lax.broadcasted_iota(jnp.int32, (L, L), 1)
    tril = row >= col
    qb, kb, vb = _bf(q), _bf(k), _bf(v)
    cb = _bf(c_sc[...])
    n_prev = n_sc[...]

    h_all = jnp.zeros((L, W), F32)
    w_all = jnp.zeros((L, W), F32)
    cdec_all = jnp.zeros((1, W), F32)
    for h in range(H_MLSTM):
        mk = _head_mask(W, h)
        b_c = cum_c[:, G_MF + h:G_MF + h + 1]
        b_r = cum_r[H_MLSTM + h:H_MLSTM + h + 1, :]
        ig_c = jnp.where(valid_c, zg[:, G_MI + h:G_MI + h + 1], NEG_INF)
        ig_r = jnp.where(valid_r, gt[h:h + 1, :], NEG_INF)
        m_prev = m_sc[h:h + 1, 0:1]
        dlog = jnp.where(tril, b_c - b_r + ig_r, NEG_INF)
        inter = b_c + m_prev
        mt = jnp.maximum(inter, jnp.max(dlog, axis=1, keepdims=True))
        dmat = jnp.exp(dlog - mt)
        qh = jnp.where(mk, q, 0.0)
        qhb = _bf(qh)
        s = _dot_nt(qhb, kb) * dmat
        dec = jnp.exp(inter - mt)
        num = _dot(_bf(s), vb) + dec * _dot(qhb, cb)
        den = jnp.sum(s, axis=1, keepdims=True) + dec * jnp.sum(qh * n_prev, axis=1, keepdims=True)
        hh = num / jnp.maximum(jnp.abs(den), jnp.exp(-mt))
        h_all = jnp.where(mk, hh, h_all)
        m_new = mt[L - 1:L, :]
        b_last = b_c[L - 1:L, :]
        w_c = jnp.exp(b_last - b_c + ig_c - m_new)
        cdec = jnp.exp(b_last + m_prev - m_new)
        w_all = jnp.where(mk, w_c, w_all)
        cdec_all = jnp.where(mk, cdec, cdec_all)
        m_sc[h:h + 1, :] = jnp.broadcast_to(m_new, (1, LANES))

    kw = k * w_all
    r2 = lax.broadcasted_iota(jnp.int32, (W, W), 0) // HEAD_DIM
    c2 = lax.broadcasted_iota(jnp.int32, (W, W), 1) // HEAD_DIM
    c_sc[...] = jnp.where(r2 == c2, cdec_all * c_sc[...] + _dot_tn(_bf(kw), vb), 0.0)
    n_sc[...] = cdec_all * n_prev + jnp.sum(kw, axis=0, keepdims=True)

    hm_ref[...] = _head_rmsnorm(h_all, gh_ref[...], H_MLSTM) * jax.nn.sigmoid(o_ref[...])


def _mlstm(zm3, zg3, gt3, c0, n0, m0, gh, n_valid):
    b, t, _ = zm3.shape
    chunk = MLSTM_CHUNK if t % MLSTM_CHUNK == 0 else t
    nc = t // chunk
    w = H_MLSTM * HEAD_DIM
    ns = MLSTM_SEQS if nc == 1 else 1
    while b % ns:
        ns //= 2
    col = lambda j: pl.BlockSpec((ns, chunk, w), lambda bi, ci: (bi, ci, j))
    state = lambda shape: pl.BlockSpec((ns,) + shape, lambda bi, ci: (bi, 0, 0))
    return pl.pallas_call(
        functools.partial(_mlstm_kernel, chunk=chunk, n_valid=n_valid, n_seq=ns),
        grid=(b // ns, nc),
        in_specs=[col(0), col(1), col(2), col(3),
                  pl.BlockSpec((ns, chunk, LANES), lambda bi, ci: (bi, ci, 0)),
                  pl.BlockSpec((ns, G_ROWS, chunk), lambda bi, ci: (bi, 0, ci)),
                  state((w, HEAD_DIM)), state((1, w)), state((SUBLANES, LANES)),
                  _const_spec((1, w))],
        out_specs=[pl.BlockSpec((ns, chunk, w), lambda bi, ci: (bi, ci, 0)),
                   state((w, HEAD_DIM)), state((1, w)), state((SUBLANES, LANES))],
        out_shape=[jax.ShapeDtypeStruct((b, t, w), F32),
                   jax.ShapeDtypeStruct((b, w, HEAD_DIM), F32),
                   jax.ShapeDtypeStruct((b, 1, w), F32),
                   jax.ShapeDtypeStruct((b, SUBLANES, LANES), F32)],
        scratch_shapes=[pltpu.VMEM((ns, w, w), F32), pltpu.VMEM((ns, 1, w), F32),
                        pltpu.VMEM((ns, SUBLANES, LANES), F32)],
        compiler_params=_params(("parallel", "arbitrary")),
        name="mlstm",
    )(zm3, zm3, zm3, zm3, zg3, gt3, c0, n0, m0, gh)


def _lane_tile(x, n):
    return x if n == 1 else jnp.concatenate([x] * n, axis=1)


def _staged(n, lag, *stages):
    for i in range(n + lag * (len(stages) - 1)):
        for s, stage in enumerate(stages):
            if 0 <= i - s * lag < n:
                stage(i - s * lag)


def _fox_kernel(q_ref, k_ref, v_ref, gt_ref, gh_ref, o_ref, fc_sc, qh_sc, acc_sc, m_sc, l_sc, *stage_sc, tq, tk):
    s_sc, p_sc, al_sc = stage_sc[0:H_FOX], stage_sc[H_FOX:2 * H_FOX], stage_sc[2 * H_FOX:]
    W = H_FOX * HEAD_DIM
    nt = tk // LANES
    qi = pl.program_id(1)
    n_full = (qi * tq) // tk

    @pl.when(qi == 0)
    def _():
        t_all = gt_ref.shape[2]
        chunk = min(t_all, 512)
        triu = _tri(chunk, False)
        carry = jnp.zeros((G_ROWS, 1), F32)
        for c in range(t_all // chunk):
            cum = _cumsum_lanes(gt_ref[0, :, c * chunk:(c + 1) * chunk], triu) + carry
            fc_sc[:, c * chunk:(c + 1) * chunk] = cum * LOG2E
            carry = cum[:, chunk - 1:chunk]

    q = q_ref[0]
    for h in range(H_FOX):
        qh_sc[h] = _bf(jnp.where(_head_mask(W, h), q, 0.0))
    acc_sc[...] = jnp.zeros_like(acc_sc)
    m_sc[...] = jnp.full_like(m_sc, NEG_INF)
    l_sc[...] = jnp.zeros_like(l_sc)

    def step(j, masked):
        off = pl.multiple_of(j * tk, tk)
        kb = k_ref[0, pl.ds(off, tk), :]
        vb = v_ref[0, pl.ds(off, tk), :]
        fk = fc_sc[:, pl.ds(off, tk)]

        def causal(r0, shape):
            qpos = qi * tq + r0 + lax.broadcasted_iota(jnp.int32, shape, 0)
            return off + lax.broadcasted_iota(jnp.int32, shape, 1) <= qpos

        def scores(h):
            s_sc[h][...] = _dot_nt(qh_sc[h], kb)

        def softmax(h):
            t = s_sc[h][...] * (HEAD_DIM ** -0.5 * LOG2E) - fk[2 * H_MLSTM + h:2 * H_MLSTM + h + 1, :]
            if masked:
                t = jnp.where(causal(0, t.shape), t, NEG_INF)
            m_prev = m_sc[h]
            m_new = jnp.maximum(m_prev, jnp.max(t, axis=1, keepdims=True))
            alpha = jnp.exp2(m_prev - m_new)
            p = jnp.exp2(t - _lane_tile(m_new, nt))
            l_sc[h] = alpha * l_sc[h] + jnp.sum(p, axis=1, keepdims=True)
            m_sc[h] = m_new
            al_sc[h][...] = alpha
            p_sc[h][...] = _bf(p)

        def values(h):
            acc_sc[h] = _lane_tile(al_sc[h][...], W // LANES) * acc_sc[h] + _dot(p_sc[h][...], vb)

        _staged(H_FOX, STAGE_LAG, scores, softmax, values)

    def full_step(j, carry):
        step(j, False)
        return carry

    lax.fori_loop(0, n_full, full_step, 0)
    step(n_full, True)

    out = jnp.zeros((tq, W), F32)
    for h in range(H_FOX):
        out = jnp.where(_head_mask(W, h), acc_sc[h] * _lane_tile(1.0 / l_sc[h], W // LANES), out)
    o_ref[0] = _head_rmsnorm(out, gh_ref[...], H_FOX)


def _fox_prompt(zf3, kb3, vb3, gt3, gh):
    b, t, _ = zf3.shape
    w = H_FOX * HEAD_DIM
    tq = ATT_TQ if t % ATT_TQ == 0 else t
    tk = ATT_TK if t % ATT_TK == 0 else t
    seq = lambda shape: pl.BlockSpec((1,) + shape, lambda bi, qi: (bi, 0, 0))
    return pl.pallas_call(
        functools.partial(_fox_kernel, tq=tq, tk=tk),
        grid=(b, t // tq),
        in_specs=[pl.BlockSpec((1, tq, w), lambda bi, qi: (bi, qi, 0)),
                  seq((t, w)), seq((t, w)), seq((G_ROWS, t)), _const_spec((1, w))],
        out_specs=pl.BlockSpec((1, tq, w), lambda bi, qi: (bi, qi, 0)),
        out_shape=jax.ShapeDtypeStruct((b, t, w), F32),
        scratch_shapes=([pltpu.VMEM((G_ROWS, t), F32),
                         pltpu.VMEM((H_FOX, tq, w), BF16), pltpu.VMEM((H_FOX, tq, w), F32),
                         pltpu.VMEM((H_FOX, tq, LANES), F32), pltpu.VMEM((H_FOX, tq, LANES), F32)]
                        + [pltpu.VMEM((tq, tk), F32)] * H_FOX + [pltpu.VMEM((tq, tk), BF16)] * H_FOX
                        + [pltpu.VMEM((tq, LANES), F32)] * H_FOX),
        compiler_params=_params(("parallel", "arbitrary")),
        name="fox_prompt",
    )(zf3, kb3, vb3, gt3, gh)


def _rope_swap(x, group):
    w = x.shape[1]
    half = ROPE_DIM // 2
    lane = lax.broadcasted_iota(jnp.int32, x.shape, 1)
    first = (lane % group) < half
    return jnp.where(first, pltpu.roll(x, w - half, 1), pltpu.roll(x, half, 1))


def _mla_pre_kernel(zcq_ref, zckv_ref, zg_ref, cs_ref, sn_ref, gcq_ref, wn_ref, wr_ref, gckv_ref, wuk_ref, sel_ref,
                    q_ref, lat_ref, latp_ref):
    zcq = zcq_ref[...]
    cqn = zcq * lax.rsqrt(jnp.mean(zcq * zcq, axis=1, keepdims=True) + RMS_EPS) * gcq_ref[...]
    cqb = _bf(cqn)
    qn = _dot(cqb, wn_ref[...])
    qr = _dot(cqb, wr_ref[...])
    cs = cs_ref[...]
    sn = sn_ref[...]
    qrot = qr * cs + _rope_swap(qr, ROPE_DIM) * sn
    qlat = _dot(_bf(qn), wuk_ref[...])
    qrope = _dot(_bf(qrot), sel_ref[...])
    for h in range(H_MLA):
        q_ref[0, h, :, 0:LANES] = qlat[:, h * LANES:(h + 1) * LANES].astype(q_ref.dtype)
        q_ref[0, h, :, LANES:2 * LANES] = qrope[:, h * LANES:(h + 1) * LANES].astype(q_ref.dtype)

    zckv = zckv_ref[...]
    ckvn = zckv * lax.rsqrt(jnp.mean(zckv * zckv, axis=1, keepdims=True) + RMS_EPS) * gckv_ref[...]
    zg = zg_ref[...]
    krot = zg * cs[:, :LANES] + _rope_swap(zg, ROPE_DIM) * sn[:, :LANES]
    lane = lax.broadcasted_iota(jnp.int32, krot.shape, 1)
    krz = jnp.where(lane < ROPE_DIM, krot, jnp.where(lane == LANES - 1, 1.0, 0.0))
    lat_ref[:, 0:KV_LORA] = ckvn
    lat_ref[:, KV_LORA:KV_LORA + ROPE_DIM] = krot[:, 0:ROPE_DIM]
    latp_ref[:, 0:KV_LORA] = _bf(ckvn)
    latp_ref[:, KV_LORA:2 * KV_LORA] = _bf(krz)


def _mla_pre(zcq, zckv, zg, cs, sn, lw, bq, tq, q_dtype):
    n = zcq.shape[0]
    tm = _token_tile(min(n, tq))
    tm = min(tm, tq)
    nt_tab = cs.shape[0] // tm
    nt_q = tq // tm
    row = lambda w: pl.BlockSpec((tm, w), lambda i: (i, 0))
    tab = pl.BlockSpec((tm, 2 * LANES), lambda i: (i % nt_tab, 0))
    consts = (lw["g_cq"], lw["w_uq_n"], lw["w_uq_r"], lw["g_ckv"], lw["wuk_bd"], lw["sel"])
    return pl.pallas_call(
        _mla_pre_kernel,
        grid=(n // tm,),
        in_specs=[row(256), row(KV_LORA), row(LANES), tab, tab] + [_const_spec(c.shape) for c in consts],
        out_specs=[pl.BlockSpec((1, H_MLA, tm, 2 * LANES), lambda i: (i // nt_q, 0, i % nt_q, 0)),
                   row(KV_LORA + ROPE_DIM), row(2 * LANES)],
        out_shape=[jax.ShapeDtypeStruct((bq, H_MLA, tq, 2 * LANES), q_dtype),
                   jax.ShapeDtypeStruct((n, KV_LORA + ROPE_DIM), F32),
                   jax.ShapeDtypeStruct((n, 2 * LANES), BF16)],
        compiler_params=_params(("parallel",)),
        name="mla_pre",
    )(zcq, zckv, zg, cs, sn, *consts)


def _mla_out(acc, inv_l, wuv_ref, gh, rows):
    out = jnp.zeros((rows, H_MLA * HEAD_DIM), F32)
    for h in range(H_MLA):
        o_h = acc[h * rows:(h + 1) * rows] * inv_l[h * rows:(h + 1) * rows]
        out = out + _dot(_bf(o_h), wuv_ref[h])
    return _head_rmsnorm(out, gh, H_MLA)


def _mla_kernel(q_ref, k_ref, wuv_ref, gh_ref, o_ref, acc_sc, m_sc, *stage_sc, tq, tk):
    s_sc, p_sc, al_sc = stage_sc[0:H_MLA], stage_sc[H_MLA:2 * H_MLA], stage_sc[2 * H_MLA:]
    nt = tk // LANES
    scale2 = (NOPE_DIM + ROPE_DIM) ** -0.5 * LOG2E
    qi = pl.program_id(1)
    n_full = (qi * tq) // tk
    acc_sc[...] = jnp.zeros_like(acc_sc)
    m_sc[...] = jnp.full_like(m_sc, NEG_INF)

    def step(j, masked):
        off = pl.multiple_of(j * tk, tk)
        kb = k_ref[0, pl.ds(off, tk), :]

        def causal(r0, shape):
            qpos = qi * tq + r0 + lax.broadcasted_iota(jnp.int32, shape, 0)
            return off + lax.broadcasted_iota(jnp.int32, shape, 1) <= qpos
        def scores(h):
            s_sc[h][...] = _dot_nt(q_ref[0, h], kb)

        def softmax(h):
            s = s_sc[h][...]
            if masked:
                s = jnp.where(causal(0, s.shape), s, NEG_INF)
            m_prev = m_sc[h]
            m_new = jnp.maximum(m_prev, jnp.max(s, axis=1, keepdims=True))
            al_sc[h][...] = jnp.exp2((m_prev - m_new) * scale2)
            p_sc[h][...] = _bf(jnp.exp2((s - _lane_tile(m_new, nt)) * scale2))
            m_sc[h] = m_new

        def values(h):
            acc_sc[h] = _lane_tile(al_sc[h][...], 2) * acc_sc[h] + _dot(p_sc[h][...], kb)

        _staged(H_MLA, STAGE_LAG, scores, softmax, values)

    def full_step(j, carry):
        step(j, False)
        return carry

    lax.fori_loop(0, n_full, full_step, 0)
    step(n_full, True)

    out = jnp.zeros((tq, H_MLA * HEAD_DIM), F32)
    for h in range(H_MLA):
        acc = acc_sc[h]
        o_h = acc[:, :KV_LORA] * (1.0 / acc[:, 2 * LANES - 1:2 * LANES])
        out = out + _dot(_bf(o_h), wuv_ref[h])
    o_ref[0] = _head_rmsnorm(out, gh_ref[...], H_MLA)


def _mla_prompt(qp, latp3, wuv_pad, gh):
    b, _, t, _ = qp.shape
    tq = ATT_TQ if t % ATT_TQ == 0 else t
    tk = ATT_TK if t % ATT_TK == 0 else t
    w = H_MLA * HEAD_DIM
    return pl.pallas_call(
        functools.partial(_mla_kernel, tq=tq, tk=tk),
        grid=(b, t // tq),
        in_specs=[pl.BlockSpec((1, H_MLA, tq, 2 * LANES), lambda bi, qi: (bi, 0, qi, 0)),
                  pl.BlockSpec((1, t, 2 * LANES), lambda bi, qi: (bi, 0, 0)),
                  _const_spec(wuv_pad.shape), _const_spec((1, w))],
        out_specs=pl.BlockSpec((1, tq, w), lambda bi, qi: (bi, qi, 0)),
        out_shape=jax.ShapeDtypeStruct((b, t, w), F32),
        scratch_shapes=([pltpu.VMEM((H_MLA, tq, 2 * LANES), F32), pltpu.VMEM((H_MLA, tq, LANES), F32)]
                        + [pltpu.VMEM((tq, tk), F32)] * H_MLA + [pltpu.VMEM((tq, tk), BF16)] * H_MLA
                        + [pltpu.VMEM((tq, LANES), F32)] * H_MLA),
        compiler_params=_params(("parallel", "arbitrary")),
        name="mla_prompt",
    )(qp, latp3, wuv_pad, gh)


def _decode_kernel(pt_ref, kt_hbm, vt_hbm, lt_hbm, lf_hbm, kn_ref, vn_ref, ln_ref, gtn_ref, qf_ref, qa_ref,
                   wuv_ref, ghf_ref, gha_ref, of_ref, oa_ref,
                   k_buf, v_buf, l_buf, f_buf, sems, lat16, facc, fm, fl, aacc, am, al, carry,
                   *, n_group, layer, n_new):
    G = n_group
    bi = pl.program_id(0)
    j = pl.program_id(1)
    nj = pl.num_programs(1)
    step = bi * nj + j
    last_step = pl.num_programs(0) * nj - 1
    WF = H_FOX * HEAD_DIM
    RF = H_FOX * PAD_T
    RA = H_MLA * PAD_T

    def page_copies(s, slot):
        b_s = s // nj
        first = (s % nj) * 2 * G + slot * G
        out = []
        for g in range(G):
            page = pt_ref[b_s, first + g]
            out.append(pltpu.make_async_copy(kt_hbm.at[layer, page], k_buf.at[slot, g], sems.at[slot, 0]))
            out.append(pltpu.make_async_copy(vt_hbm.at[layer, page], v_buf.at[slot, g], sems.at[slot, 1]))
            out.append(pltpu.make_async_copy(lt_hbm.at[layer, page], l_buf.at[slot, g], sems.at[slot, 2]))
            out.append(pltpu.make_async_copy(lf_hbm.at[layer, page],
                                             f_buf.at[slot, pl.ds(g * SUBLANES, H_FOX), :], sems.at[slot, 3]))
        return out

    @pl.when(step == 0)
    def _():
        for slot in range(2):
            for g in range(G):
                f_buf[slot, g * SUBLANES + H_FOX:(g + 1) * SUBLANES, :] = jnp.zeros((SUBLANES - H_FOX, LANES), F32)
        lat16[...] = jnp.zeros_like(lat16)
        for slot in range(2):
            for c in page_copies(step, slot):
                c.start()

    @pl.when(j == 0)
    def _():
        facc[...] = jnp.zeros_like(facc)
        fm[...] = jnp.full_like(fm, NEG_INF)
        fl[...] = jnp.zeros_like(fl)
        aacc[...] = jnp.zeros_like(aacc)
        am[...] = jnp.full_like(am, NEG_INF)
        al[...] = jnp.zeros_like(al)
        carry[...] = jnp.zeros_like(carry)

    q = qf_ref[0]
    qbd = jnp.concatenate([_bf(jnp.where(_head_mask(WF, h), q, 0.0)) for h in range(H_FOX)], axis=0)
    qa = _bf(qa_ref[0].reshape(RA, 2 * LANES))
    triu = _tri(LANES, False)
    scale_f = HEAD_DIM ** -0.5 * LOG2E
    scale_a = (NOPE_DIM + ROPE_DIM) ** -0.5 * LOG2E

    def like_keys(m, width):
        return _lane_tile(m, width // LANES) if width % LANES == 0 else m[:, :width]

    def merge(s_f, s_a, pv_f, pv_a):
        width = s_f.shape[1]
        m_prev = fm[...]
        m_new = jnp.maximum(m_prev, jnp.max(s_f, axis=1, keepdims=True))
        alpha = jnp.exp2(m_prev - m_new)
        p = jnp.exp2(s_f - like_keys(m_new, width))
        fl[...] = alpha * fl[...] + jnp.sum(p, axis=1, keepdims=True)
        fm[...] = m_new
        facc[...] = _lane_tile(alpha, WF // LANES) * facc[...] + pv_f(_bf(p))
        m_prev = am[...]
        m_new = jnp.maximum(m_prev, jnp.max(s_a, axis=1, keepdims=True))
        alpha = jnp.exp2((m_prev - m_new) * scale_a)
        p = jnp.exp2((s_a - like_keys(m_new, width)) * scale_a)
        al[...] = alpha * al[...] + jnp.sum(p, axis=1, keepdims=True)
        am[...] = m_new
        aacc[...] = alpha * aacc[...] + pv_a(_bf(p))

    def head_rows(cum, g0):
        return jnp.concatenate([jnp.broadcast_to(cum[g0 + h:g0 + h + 1, :], (PAD_T, cum.shape[1]))
                                for h in range(H_FOX)], axis=0)

    def block(kts, vts, lts, lf8, kbuf):
        n = len(kts)
        cs = _cumsum_lanes(lf8, triu)
        tot = jnp.broadcast_to(cs[:, LANES - 1:LANES], cs.shape)
        base = jnp.concatenate([carry[...]] * n, axis=0)
        if n > 1:
            r = lax.broadcasted_iota(jnp.int32, (n * SUBLANES, n * SUBLANES), 0)
            c = lax.broadcasted_iota(jnp.int32, (n * SUBLANES, n * SUBLANES), 1)
            earlier = jnp.where((r % SUBLANES == c % SUBLANES) & (c // SUBLANES < r // SUBLANES), 1.0, 0.0)
            base = base + _cumsum_rows(earlier.astype(BF16), tot)
        last = slice((n - 1) * SUBLANES, n * SUBLANES)
        carry[...] = base[last] + tot[last]
        cum = (cs + base) * LOG2E
        s_f, s_a = [], []
        for g in range(n):
            s_f.append(_dot(qbd, _bf(kts[g][...])) * scale_f - head_rows(cum, g * SUBLANES))
            kbuf[g, 0:KV_LORA + ROPE_DIM, :] = _bf(lts[g][...])
            s_a.append(_dot(qa, kbuf[g]))

        def pv_f(p):
            return sum(_dot_nt(p[:, g * LANES:(g + 1) * LANES], _bf(vts[g][...])) for g in range(n))

        def pv_a(p):
            return sum(_dot_nt(p[:, g * LANES:(g + 1) * LANES], kbuf[g, 0:KV_LORA, :]) for g in range(n))

        merge(jnp.concatenate(s_f, axis=1), jnp.concatenate(s_a, axis=1), pv_f, pv_a)

    def new_token_block():
        kn, vn, ln = _bf(kn_ref[0]), _bf(vn_ref[0]), ln_ref[0]
        valid = lax.broadcasted_iota(jnp.int32, (1, PAD_T), 1) < n_new
        cs = _cumsum_lanes(jnp.where(valid, gtn_ref[0], 0.0), _tri(PAD_T, False))
        cum = (cs[2 * H_MLSTM:2 * H_MLSTM + H_FOX] + carry[0:H_FOX, 0:PAD_T]) * LOG2E
        s_f = _dot_nt(qbd, kn) * scale_f - head_rows(cum, 0)
        s_a = _dot_nt(qa, ln)
        ok_f = lax.broadcasted_iota(jnp.int32, s_f.shape, 1) <= lax.broadcasted_iota(jnp.int32, s_f.shape, 0) % PAD_T
        ok_a = lax.broadcasted_iota(jnp.int32, s_a.shape, 1) <= lax.broadcasted_iota(jnp.int32, s_a.shape, 0) % PAD_T
        merge(jnp.where(ok_f, s_f, NEG_INF), jnp.where(ok_a, s_a, NEG_INF),
              lambda p: _dot(p, vn), lambda p: _dot(p, ln[:, :KV_LORA]))

    def half_step(slot):
        for c in page_copies(step, slot):
            c.wait()
        block([k_buf.at[slot, g] for g in range(G)], [v_buf.at[slot, g] for g in range(G)],
              [l_buf.at[slot, g] for g in range(G)], f_buf[slot], lat16.at[slot])

    nxt = jnp.minimum(step + 1, last_step)
    half_step(0)
    for c in page_copies(nxt, 0):
        c.start()
    half_step(1)
    for c in page_copies(nxt, 1):
        c.start()

    @pl.when(step == last_step)
    def _():
        for slot in range(2):
            for c in page_copies(step, slot):
                c.wait()

    @pl.when(j == nj - 1)
    def _():
        new_token_block()
        accf = facc[...] * _lane_tile(1.0 / fl[...], WF // LANES)
        of = jnp.zeros((PAD_T, WF), F32)
        for h in range(H_FOX):
            of = jnp.where(_head_mask(WF, h), accf[h * PAD_T:(h + 1) * PAD_T], of)
        of_ref[0] = _head_rmsnorm(of, ghf_ref[...], H_FOX)
        oa_ref[0] = _mla_out(aacc[...], 1.0 / al[...], wuv_ref, gha_ref[...], PAD_T)


def _decode(page_table, layer, caches, zf3, latp3, gt3, qa, wuv_pad, ghf, gha, n_new):
    b, n_pages = page_table.shape
    assert n_pages % 2 == 0, "the two-slot page pipeline needs an even number of pages per sequence"
    G = DECODE_PAGES
    while n_pages % (2 * G):
        G //= 2
    kt, vt, lt, lf = caches
    page = kt.shape[-1]

    def seq_spec(arr):
        nd = arr.ndim
        return pl.BlockSpec((1,) + arr.shape[1:], lambda bi, ji, pt: (bi,) + (0,) * (nd - 1))

    def const(arr):
        nd = arr.ndim
        return pl.BlockSpec(arr.shape, lambda bi, ji, pt: (0,) * nd)

    wf = H_FOX * HEAD_DIM
    wa = H_MLA * HEAD_DIM
    zcol = lambda c: pl.BlockSpec((1, PAD_T, wf), lambda bi, ji, pt: (bi, 0, c))
    in_specs = [pl.BlockSpec(memory_space=pl.ANY)] * 4 + [zcol(1), zcol(2), seq_spec(latp3), seq_spec(gt3),
                                                           zcol(0), seq_spec(qa)]
    operands = [kt, vt, lt, lf, zf3, zf3, latp3, gt3, zf3, qa]
    for arr in (wuv_pad, ghf, gha):
        in_specs.append(const(arr))
        operands.append(arr)
    grid_spec = pltpu.PrefetchScalarGridSpec(
        num_scalar_prefetch=1,
        grid=(b, n_pages // (2 * G)),
        in_specs=in_specs,
        out_specs=[pl.BlockSpec((1, PAD_T, wf), lambda bi, ji, pt: (bi, 0, 0)),
                   pl.BlockSpec((1, PAD_T, wa), lambda bi, ji, pt: (bi, 0, 0))],
        scratch_shapes=[pltpu.VMEM((2, G) + kt.shape[2:], F32), pltpu.VMEM((2, G) + vt.shape[2:], F32),
                        pltpu.VMEM((2, G) + lt.shape[2:], F32), pltpu.VMEM((2, G * SUBLANES, page), F32),
                        pltpu.SemaphoreType.DMA((2, 4)),
                        pltpu.VMEM((2, G, 2 * LANES, page), BF16),
                        pltpu.VMEM((H_FOX * PAD_T, wf), F32), pltpu.VMEM((H_FOX * PAD_T, LANES), F32),
                        pltpu.VMEM((H_FOX * PAD_T, LANES), F32),
                        pltpu.VMEM((H_MLA * PAD_T, KV_LORA), F32), pltpu.VMEM((H_MLA * PAD_T, LANES), F32),
                        pltpu.VMEM((H_MLA * PAD_T, LANES), F32),
                        pltpu.VMEM((SUBLANES, LANES), F32)])
    return pl.pallas_call(
        functools.partial(_decode_kernel, n_group=G, layer=layer, n_new=n_new),
        grid_spec=grid_spec,
        out_shape=[jax.ShapeDtypeStruct((b, PAD_T, wf), F32), jax.ShapeDtypeStruct((b, PAD_T, wa), F32)],
        compiler_params=_params(("arbitrary", "arbitrary")),
        name="decode",
    )(page_table, *operands)


def _out_ffn_kernel(x_ref, hm_ref, hf_ref, ha_ref, wom_ref, wof_ref, woa_ref, g1_ref, b1_ref,
                    wg_ref, wu_ref, wd_ref, g2_ref, b2_ref, y_ref, acc_sc, *, alpha, n_chunks):
    mix = (_dot(_bf(hm_ref[...]), wom_ref[...]) + _dot(_bf(hf_ref[...]), wof_ref[...])
           + _dot(_bf(ha_ref[...]), woa_ref[...]))
    x1 = _layernorm(alpha * x_ref[...] + mix, g1_ref[...], b1_ref[...])
    x1b = _bf(x1)
    acc_sc[...] = jnp.zeros_like(acc_sc)

    for c in range(n_chunks):
        cols = slice(c * FF_CHUNK, (c + 1) * FF_CHUNK)
        g = _dot(x1b, wg_ref[:, cols])
        u = _dot(x1b, wu_ref[:, cols])
        acc_sc[...] += _dot(_bf(g * jax.nn.sigmoid(g) * u), wd_ref[cols, :])
    y_ref[...] = _layernorm(alpha * x1 + acc_sc[...], g2_ref[...], b2_ref[...])


def _out_ffn(x, hm, hf, ha, lw, alpha):
    n, d = x.shape
    tm = _token_tile(n)
    row = lambda w: pl.BlockSpec((tm, w), lambda i: (i, 0))
    consts = (lw["wo_m"], lw["wo_f"], lw["wo_a"], lw["ln1_g"], lw["ln1_b"],
              lw["wg"], lw["wu"], lw["wd"], lw["ln2_g"], lw["ln2_b"])

    def resident(c):
        nd = c.ndim
        return pl.BlockSpec(c.shape, lambda i: (0,) * nd, pipeline_mode=pl.Buffered(1))

    return pl.pallas_call(
        functools.partial(_out_ffn_kernel, alpha=alpha, n_chunks=lw["wg"].shape[1] // FF_CHUNK),
        grid=(n // tm,),
        in_specs=[row(d), row(hm.shape[1]), row(hf.shape[1]), row(ha.shape[1])] + [resident(c) for c in consts],
        out_specs=row(d),
        out_shape=jax.ShapeDtypeStruct((n, d), F32),
        scratch_shapes=[pltpu.VMEM((tm, d), F32)],
        compiler_params=_params(("parallel",)),
        name="out_ffn",
    )(x, hm, hf, ha, *consts)


FF_CHUNK = 256


def _prep_layer(l, w_in, b_mi, b_mf, b_ff, g_cq, w_uq, g_ckv, w_uk, w_uv, g_head, w_out,
                ln1_g, ln1_b, w_gate, w_up, w_down, ln2_g, ln2_b):
    d = w_in.shape[1]
    wm = H_MLSTM * HEAD_DIM
    wf = H_FOX * HEAD_DIM
    q_lora = g_cq.shape[1]
    split = (wm,) * 4 + (H_MLSTM, H_MLSTM) + (wf,) * 3 + (H_FOX,) + (q_lora, KV_LORA, ROPE_DIM)
    at = [int(i) for i in np.cumsum(split)[:-1]]
    mq, mk, mv, mo, mi, mf, fq, fk, fv, ff, cq, ckv, kr = jnp.split(w_in[l], at, axis=1)
    pad = jnp.zeros((d, LANES - ROPE_DIM - 2 * H_MLSTM - H_FOX), F32)
    w_perm = _bf(jnp.concatenate([mq, mk, mv, mo, fq, fk, fv, cq, ckv, kr, mi, mf, ff, pad], axis=1))
    gbias = jnp.zeros((1, LANES), F32)
    gbias = gbias.at[0, G_MI:G_MI + H_MLSTM].set(b_mi[l]).at[0, G_MF:G_MF + H_MLSTM].set(b_mf[l])
    gbias = gbias.at[0, G_FF:G_FF + H_FOX].set(b_ff[l])
    uq = w_uq[l].reshape(q_lora, H_MLA, NOPE_DIM + ROPE_DIM)
    eye = jnp.eye(H_MLA, dtype=F32)
    sel = np.zeros((H_MLA * ROPE_DIM, H_MLA * LANES), np.float32)
    for h in range(H_MLA):
        sel[h * ROPE_DIM + np.arange(ROPE_DIM), h * LANES + np.arange(ROPE_DIM)] = 1.0
    assert w_gate.shape[2] % FF_CHUNK == 0
    return dict(
        w_perm=w_perm, gbias=gbias,
        g_cq=g_cq[l][None], g_ckv=g_ckv[l][None],
        w_uq_n=_bf(uq[:, :, :NOPE_DIM].reshape(q_lora, H_MLA * NOPE_DIM)),
        w_uq_r=_bf(uq[:, :, NOPE_DIM:].reshape(q_lora, H_MLA * ROPE_DIM)),
        wuk_bd=_bf(jnp.einsum("chd,hg->hdgc", w_uk[l], eye).reshape(H_MLA * NOPE_DIM, H_MLA * KV_LORA)),
        sel=jnp.asarray(sel, BF16),
        wuv_pad=_bf(jnp.einsum("chd,hg->hcgd", w_uv[l], eye).reshape(H_MLA, KV_LORA, H_MLA * HEAD_DIM)),
        gh_m=g_head[l][None, :wm], gh_f=g_head[l][None, wm:wm + wf], gh_a=g_head[l][None, wm + wf:],
        wo_m=_bf(w_out[l][:wm]), wo_f=_bf(w_out[l][wm:wm + wf]), wo_a=_bf(w_out[l][wm + wf:]),
        ln1_g=ln1_g[l][None], ln1_b=ln1_b[l][None], ln2_g=ln2_g[l][None], ln2_b=ln2_b[l][None],
        wg=_bf(w_gate[l]), wu=_bf(w_up[l]), wd=_bf(w_down[l]),
    )


def _rope_tables(pos):
    half = ROPE_DIM // 2
    freq = ROPE_THETA ** (-jnp.arange(half, dtype=F32) * (2.0 / ROPE_DIM))
    ang = pos.astype(F32)[:, None] * freq[None, :]
    cos, sin = jnp.cos(ang), jnp.sin(ang)
    reps = 2 * LANES // ROPE_DIM
    return (jnp.tile(jnp.concatenate([cos, cos], axis=1), (1, reps)),
            jnp.tile(jnp.concatenate([-sin, sin], axis=1), (1, reps)))


def _layer_common(x3, lw, cs, sn, c0, n0, m0, n_valid, mla_bq, mla_tq, q_dtype):
    b, t, d = x3.shape
    zm, zf, zcq, zckv, zg, kb, vb = _inproj(x3.reshape(b * t, d), lw["w_perm"], lw["gbias"])
    zg3 = zg.reshape(b, t, LANES)
    gt3 = jnp.swapaxes(zg3[:, :, G_MI:G_MI + G_ROWS], 1, 2)
    hm, c1, n1, m1 = _mlstm(zm.reshape(b, t, -1), zg3, gt3, c0, n0, m0, lw["gh_m"], n_valid)
    qp, lat, latp = _mla_pre(zcq, zckv, zg, cs, sn, lw, mla_bq, mla_tq, q_dtype)
    return zf, (kb, vb), zg3, gt3, hm, (c1, n1, m1), qp, lat, latp


def _state_out(c1, n1, m1):
    b = c1.shape[0]
    return (c1.reshape(b, H_MLSTM, HEAD_DIM, HEAD_DIM), n1.reshape(b, H_MLSTM, HEAD_DIM), m1[:, :H_MLSTM, 0])


def kernel(x_prompt, x_sample, cache_fox_k, cache_fox_v, cache_fox_logf, cache_mla_latent, state_mlstm_C, state_mlstm_n, state_mlstm_m, page_table, w_in, b_mlstm_i, b_mlstm_f, b_fox_f, g_cq, w_uq, g_ckv, w_uk, w_uv, g_head, w_out, ln1_g, ln1_b, w_gate, w_up, w_down, ln2_g, ln2_b):
    depth = w_in.shape[0]
    bp, tp, d = x_prompt.shape
    bs, ts, _ = x_sample.shape
    n_pages = page_table.shape[1]
    page = cache_fox_k.shape[2]
    past = n_pages * page
    alpha = (2 * depth) ** 0.25
    wm = H_MLSTM * HEAD_DIM
    wf = H_FOX * HEAD_DIM

    pools = (jnp.transpose(cache_fox_k, (0, 1, 3, 4, 2)).reshape(depth, -1, wf, page),
             jnp.transpose(cache_fox_v, (0, 1, 3, 4, 2)).reshape(depth, -1, wf, page),
             jnp.transpose(cache_mla_latent, (0, 1, 3, 2)),
             jnp.transpose(cache_fox_logf, (0, 1, 3, 2)))

    cs_p, sn_p = _rope_tables(jnp.arange(tp, dtype=jnp.int32))
    cs_s, sn_s = _rope_tables(past + jnp.arange(PAD_T, dtype=jnp.int32))
    cs_s, sn_s = jnp.tile(cs_s, (bs, 1)), jnp.tile(sn_s, (bs, 1))

    yp = x_prompt
    ys = jnp.pad(x_sample, ((0, 0), (0, PAD_T - ts), (0, 0)))
    outs_p, outs_s = [], []
    for l in range(depth):
        lw = _prep_layer(l, w_in, b_mlstm_i, b_mlstm_f, b_fox_f, g_cq, w_uq, g_ckv, w_uk, w_uv, g_head, w_out,
                         ln1_g, ln1_b, w_gate, w_up, w_down, ln2_g, ln2_b)

        c0 = jnp.zeros((bp, wm, HEAD_DIM), F32)
        n0 = jnp.zeros((bp, 1, wm), F32)
        m0 = jnp.zeros((bp, SUBLANES, LANES), F32)
        zf, (kb, vb), zg3, gt3, hm, st, qp, lat, latp = _layer_common(yp, lw, cs_p, sn_p, c0, n0, m0, tp, bp, tp, BF16)
        zf3 = zf.reshape(bp, tp, -1)
        hf = _fox_prompt(zf3, kb.reshape(bp, tp, -1), vb.reshape(bp, tp, -1), gt3, lw["gh_f"])
        ha = _mla_prompt(qp, latp.reshape(bp, tp, -1), lw["wuv_pad"], lw["gh_a"])
        yp = _out_ffn(yp.reshape(bp * tp, d), hm.reshape(bp * tp, -1), hf.reshape(bp * tp, -1),
                      ha.reshape(bp * tp, -1), lw, alpha).reshape(bp, tp, d)
        outs_p.append((zf3[:, :, wf:2 * wf].reshape(bp, tp, H_FOX, HEAD_DIM),
                       zf3[:, :, 2 * wf:].reshape(bp, tp, H_FOX, HEAD_DIM),
                       zg3[:, :, G_FF:G_FF + H_FOX],
                       lat.reshape(bp, tp, -1)) + _state_out(*st))

        c0 = state_mlstm_C[l].reshape(bs, wm, HEAD_DIM)
        n0 = state_mlstm_n[l].reshape(bs, 1, wm)
        m0 = jnp.broadcast_to(jnp.pad(state_mlstm_m[l], ((0, 0), (0, SUBLANES - H_MLSTM)))[:, :, None],
                              (bs, SUBLANES, LANES))
        zf, _, zg3, gt3, hm, st, qa, lat, latp = _layer_common(ys, lw, cs_s, sn_s, c0, n0, m0, ts, 1, bs * PAD_T, F32)
        zf3 = zf.reshape(bs, PAD_T, -1)
        qa4 = jnp.swapaxes(qa.reshape(H_MLA, bs, PAD_T, 2 * LANES), 0, 1)
        hf, ha = _decode(page_table, l, pools, zf3, latp.reshape(bs, PAD_T, -1), gt3, qa4,
                         lw["wuv_pad"], lw["gh_f"], lw["gh_a"], ts)
        ys = _out_ffn(ys.reshape(bs * PAD_T, d), hm.reshape(bs * PAD_T, -1), hf.reshape(bs * PAD_T, -1),
                      ha.reshape(bs * PAD_T, -1), lw, alpha).reshape(bs, PAD_T, d)
        outs_s.append((zf3[:, :ts, wf:2 * wf].reshape(bs, ts, H_FOX, HEAD_DIM),
                       zf3[:, :ts, 2 * wf:].reshape(bs, ts, H_FOX, HEAD_DIM),
                       zg3[:, :ts, G_FF:G_FF + H_FOX],
                       lat.reshape(bs, PAD_T, -1)[:, :ts]) + _state_out(*st))

    sp = [jnp.stack(a) for a in zip(*outs_p)]
    ss = [jnp.stack(a) for a in zip(*outs_s)]
    return (yp, ys[:, :ts], sp[0], sp[1], sp[2], sp[3], sp[4], sp[5], sp[6],
            ss[0], ss[1], ss[2], ss[3], ss[4], ss[5], ss[6])
```
